```python
import math
import jax, jax.numpy as jnp
from jax import lax
import numpy as np

D_MODEL = 1024
BATCH = 8
SEQ = 4096
DEPTH = 1

CHUNK = 64
N_META = 16
DN_HEADS = 8
DN_DK = 128
DN_DV = 256
DN_CONV = 4
DN_QK = DN_HEADS * DN_DK
DN_V = DN_HEADS * DN_DV
SB_HEADS = 8
SB_DH = 128
SB_W = SB_HEADS * SB_DH
SB_BLOCK = 128
D_FF = -(-8 * D_MODEL // (3 * 256)) * 256
PROJ_WIDTH = 2 * DN_QK + 2 * DN_V + 2 * DN_HEADS + 3 * SB_W + 2 * D_MODEL
RMS_EPS = 1e-6
L2_EPS = 1e-6

kernel_name = 'hybrid_gdn_stickbreak_block'


def _split_points():
    widths = (DN_QK, DN_QK, DN_V, DN_V, DN_HEADS, DN_HEADS, SB_W, SB_W, SB_W, D_MODEL, D_MODEL)
    return [int(s) for s in np.cumsum(widths)[:-1]]


def rmsnorm(x, gain):
    xf = x.astype(jnp.float32)
    y = xf * lax.rsqrt(jnp.mean(xf * xf, axis=-1, keepdims=True) + RMS_EPS)
    return (y * gain.astype(jnp.float32)).astype(x.dtype)


def l2norm(x):
    xf = x.astype(jnp.float32)
    return xf * lax.rsqrt(jnp.sum(xf * xf, axis=-1, keepdims=True) + L2_EPS)


def causal_depthwise_conv(x, w):
    K, C = w.shape
    return lax.conv_general_dilated(
        x, w[:, None, :].astype(x.dtype), window_strides=(1,), padding=[(K - 1, 0)],
        dimension_numbers=('NWC', 'WIO', 'NWC'), feature_group_count=C)


def gated_delta_rule(q, k, v, g, beta):
    B, T, H, DK = q.shape
    DV = v.shape[-1]
    N = T // CHUNK
    f32 = jnp.float32

    def to_chunks(a):
        a = a.astype(f32).reshape((B, N, CHUNK, H) + a.shape[3:])
        return jnp.moveaxis(a, (1, 3), (0, 2))

    q = to_chunks(q) * (DK ** -0.5)
    k = to_chunks(k)
    v = to_chunks(v)
    beta = to_chunks(beta)
    g = jnp.cumsum(to_chunks(g), axis=-1)
    idx = jnp.arange(CHUNK)
    incl = idx[:, None] >= idx[None, :]
    strict = idx[:, None] > idx[None, :]
    decay = jnp.exp(jnp.where(incl, g[..., :, None] - g[..., None, :], -jnp.inf))
    kb = k * beta[..., None]
    lower = jnp.where(strict, jnp.einsum('nbhid,nbhjd->nbhij', kb, k) * decay, 0.0)
    eye = jnp.eye(CHUNK, dtype=f32)
    rhs = jnp.concatenate([v * beta[..., None], kb * jnp.exp(g)[..., None]], axis=-1)
    sol = lax.linalg.triangular_solve(eye + lower, rhs, left_side=True, lower=True)
    u, w = sol[..., :DV], sol[..., DV:]
    attn = jnp.einsum('nbhid,nbhjd->nbhij', q, k) * decay
    q_dec = q * jnp.exp(g)[..., None]
    k_dec = k * jnp.exp(g[..., -1:] - g)[..., None]
    g_last = jnp.exp(g[..., -1])

    def step(S, xs):
        u_c, w_c, attn_c, qd_c, kd_c, gl_c = xs
        v_new = u_c - jnp.einsum('bhcd,bhde->bhce', w_c, S)
        o = jnp.einsum('bhcd,bhde->bhce', qd_c, S) + jnp.einsum('bhij,bhje->bhie', attn_c, v_new)
        S = S * gl_c[..., None, None] + jnp.einsum('bhcd,bhce->bhde', kd_c, v_new)
        return S, o

    S0 = jnp.zeros((B, H, DK, DV), f32)
    _, o = lax.scan(step, S0, (u, w, attn, q_dec, k_dec, g_last))
    return jnp.moveaxis(o, (0, 2), (1, 3)).reshape(B, T, H, DV)


def stick_breaking_attention(q, k, v):
    B, T, H, D = q.shape
    nq = -(-T // SB_BLOCK)
    Tp = nq * SB_BLOCK
    pad = ((0, 0), (0, Tp - T), (0, 0), (0, 0))
    f32 = jnp.float32
    qh = jnp.pad(q.astype(f32), pad).reshape(B, nq, SB_BLOCK, H, D).transpose(1, 0, 3, 2, 4)
    kh = jnp.pad(k.astype(f32), pad).transpose(0, 2, 1, 3)
    vh = jnp.pad(v.astype(f32), pad).transpose(0, 2, 1, 3)
    key_pos = jnp.arange(Tp)
    scale = D ** -0.5

    def block(args):
        q_blk, start = args
        z = jnp.einsum('bhqd,bhkd->bhqk', q_blk, kh) * scale
        q_pos = start + jnp.arange(SB_BLOCK)
        visible = key_pos[None, :] < q_pos[:, None]
        log_keep = jnp.where(visible, jax.nn.log_sigmoid(-z), 0.0)
        log_w = jax.nn.log_sigmoid(z) + lax.cumsum(log_keep, axis=3, reverse=True) - log_keep
        w = jnp.where(visible, jnp.exp(log_w), 0.0)
        return jnp.einsum('bhqk,bhkd->bhqd', w, vh)

    o = lax.map(block, (qh, jnp.arange(nq) * SB_BLOCK))
    return o.transpose(1, 0, 3, 2, 4).reshape(B, Tp, H, D)[:, :T]


def hybrid_mixer(hn, w_in, conv_q, conv_k, conv_v, a_log, dt_bias, dn_gain, sbq_gain, sbk_gain,
                 w_branch_dn, w_branch_sb, w_out):
    B, T, _ = hn.shape
    proj = hn @ w_in
    (dq, dk, dv, dz, da, db, sq, sk, sv, gate_dn, gate_sb) = jnp.split(proj, _split_points(), axis=-1)

    def heads(a, d):
        return a.reshape(B, T, -1, d)

    q = l2norm(heads(jax.nn.silu(causal_depthwise_conv(dq, conv_q)), DN_DK))
    k = l2norm(heads(jax.nn.silu(causal_depthwise_conv(dk, conv_k)), DN_DK))
    v = heads(jax.nn.silu(causal_depthwise_conv(dv, conv_v)), DN_DV)
    g = -jnp.exp(a_log.astype(jnp.float32)) * jax.nn.softplus(da.astype(jnp.float32) + dt_bias.astype(jnp.float32))
    beta = jax.nn.sigmoid(db.astype(jnp.float32))
    pad_l = (-N_META) % CHUNK
    pad_r = (-(T + pad_l)) % CHUNK

    def chunk_pad(a):
        return jnp.pad(a, ((0, 0), (pad_l, pad_r)) + ((0, 0),) * (a.ndim - 2))

    o_dn = gated_delta_rule(chunk_pad(q), chunk_pad(k), chunk_pad(v), chunk_pad(g), chunk_pad(beta))
    o_dn = o_dn[:, pad_l:pad_l + T]
    o_dn = rmsnorm(o_dn, dn_gain) * jax.nn.silu(heads(dz, DN_DV).astype(jnp.float32))
    o_dn = o_dn.astype(hn.dtype).reshape(B, T, DN_V)

    qs = rmsnorm(heads(sq, SB_DH), sbq_gain)
    ks = rmsnorm(heads(sk, SB_DH), sbk_gain)
    o_sb = stick_breaking_attention(qs, ks, heads(sv, SB_DH)).astype(hn.dtype).reshape(B, T, SB_W)

    merged = jax.nn.sigmoid(gate_dn) * (o_dn @ w_branch_dn) + jax.nn.sigmoid(gate_sb) * (o_sb @ w_branch_sb)
    return merged @ w_out


def swiglu(hn, w_ffn_in, w_ffn_out):
    gate, up = jnp.split(hn @ w_ffn_in, 2, axis=-1)
    return (jax.nn.silu(gate) * up) @ w_ffn_out


def setup_inputs(seed: int = 0) -> dict:
    key = jax.random.key(seed)
    ks = jax.random.split(key, 20)
    f32 = jnp.float32

    def nrm(k, shape, fan_in):
        return jax.random.normal(k, shape, f32) * (fan_in ** -0.5)

    def gain(k, n):
        return 1.0 + 0.02 * jax.random.normal(k, (DEPTH, n), f32)

    dt = jnp.exp(jax.random.uniform(ks[7], (DEPTH, DN_HEADS), f32) * (math.log(0.1) - math.log(1e-3)) + math.log(1e-3))
    return {
        'x': jax.random.normal(ks[0], (BATCH, SEQ, D_MODEL), f32),
        'meta_tokens': jax.random.normal(ks[1], (N_META, D_MODEL), f32),
        'norm_mix_gain': gain(ks[2], D_MODEL),
        'w_in': nrm(ks[3], (DEPTH, D_MODEL, PROJ_WIDTH), D_MODEL),
        'conv_q': nrm(ks[4], (DEPTH, DN_CONV, DN_QK), DN_CONV),
        'conv_k': nrm(ks[5], (DEPTH, DN_CONV, DN_QK), DN_CONV),
        'conv_v': nrm(ks[6], (DEPTH, DN_CONV, DN_V), DN_CONV),
        'dn_a_log': jnp.log(jax.random.uniform(ks[8], (DEPTH, DN_HEADS), f32, 1.0, 16.0)),
        'dn_dt_bias': dt + jnp.log(-jnp.expm1(-dt)),
        'dn_out_norm_gain': gain(ks[9], DN_DV),
        'sb_q_norm_gain': gain(ks[10], SB_DH),
        'sb_k_norm_gain': gain(ks[11], SB_DH),
        'w_branch_dn': nrm(ks[12], (DEPTH, DN_V, D_MODEL), DN_V),
        'w_branch_sb': nrm(ks[13], (DEPTH, SB_W, D_MODEL), SB_W),
        'w_out': nrm(ks[14], (DEPTH, D_MODEL, D_MODEL), D_MODEL),
        'norm_ffn_gain': gain(ks[15], D_MODEL),
        'w_ffn_in': nrm(ks[16], (DEPTH, D_MODEL, 2 * D_FF), D_MODEL),
        'w_ffn_out': nrm(ks[17], (DEPTH, D_FF, D_MODEL), D_FF),
    }


def reference(x, meta_tokens, norm_mix_gain, w_in, conv_q, conv_k, conv_v, dn_a_log, dn_dt_bias,
              dn_out_norm_gain, sb_q_norm_gain, sb_k_norm_gain, w_branch_dn, w_branch_sb, w_out,
              norm_ffn_gain, w_ffn_in, w_ffn_out):
    B = x.shape[0]
    meta = jnp.broadcast_to(meta_tokens[None].astype(x.dtype), (B, N_META, D_MODEL))
    h = jnp.concatenate([meta, x], axis=1)
    for l in range(DEPTH):
        h = h + hybrid_mixer(rmsnorm(h, norm_mix_gain[l]), w_in[l], conv_q[l], conv_k[l], conv_v[l],
                             dn_a_log[l], dn_dt_bias[l], dn_out_norm_gain[l], sb_q_norm_gain[l],
                             sb_k_norm_gain[l], w_branch_dn[l], w_branch_sb[l], w_out[l])
        h = h + swiglu(rmsnorm(h, norm_ffn_gain[l]), w_ffn_in[l], w_ffn_out[l])
    return h[:, N_META:]
```

```python
import functools

import jax
import jax.numpy as jnp
from jax import lax
from jax.experimental import pallas as pl
from jax.experimental.pallas import tpu as pltpu

F32 = jnp.float32
BF16 = jnp.bfloat16

N_META = 16
CHUNK = 64
BLK = 128
N_PAD = BLK - N_META
RMS_EPS = 1e-6
L2_EPS = 1e-6
NEG_BIG = -1e30
FF_CHUNK = 256
INV_PASSES = 3
VMEM_LIMIT = 56 * 1024 * 1024

_NT = (((1,), (1,)), ((), ()))
_TN = (((0,), (0,)), ((), ()))


def _params(*sem):
    return pltpu.CompilerParams(dimension_semantics=sem, vmem_limit_bytes=VMEM_LIMIT)


def _dot(a, b, dims=None):
    if dims is None:
        return jnp.dot(a, b, preferred_element_type=F32)
    return lax.dot_general(a, b, dims, preferred_element_type=F32)


def _split(x, n):
    parts = []
    for _ in range(n - 1):
        p = x.astype(BF16)
        parts.append(p)
        x = x - p.astype(F32)
    parts.append(x.astype(BF16))
    return parts


def _dot_split(a, b, passes):
    if passes == 1:
        return _dot(a.astype(BF16), b.astype(BF16))
    ah, al = _split(a, 2)
    bh, bl = _split(b, 2)
    out = _dot(ah, bh) + _dot(al, bh)
    if passes >= 3:
        out = out + _dot(ah, bl)
    return out


def _sigmoid(x):
    return 1.0 / (1.0 + jnp.exp(-x))


def _softplus(x):
    return jnp.maximum(x, 0.0) + jnp.log(1.0 + jnp.exp(-jnp.abs(x)))


def _row_tile(n, cap=1152):
    best = 16
    for t in range(16, min(n, cap) + 1, 16):
        if n % t == 0:
            best = t
    return best


def _prenorm_kernel(x_ref, meta_ref, gain_ref, o_ref, *, n_xblk):
    i = pl.program_id(1)
    gain = gain_ref[...]

    def norm(v):
        ms = jnp.mean(v * v, axis=-1, keepdims=True)
        return (v * lax.rsqrt(ms + RMS_EPS) * gain).astype(o_ref.dtype)

    @pl.when(i < n_xblk)
    def _():
        o_ref[0] = norm(x_ref[0])

    @pl.when(i == n_xblk)
    def _():
        o_ref[0] = norm(meta_ref[...])


def _prenorm(x, meta_blk, gain):
    B, S, D = x.shape
    n_xblk = S // BLK
    return pl.pallas_call(
        functools.partial(_prenorm_kernel, n_xblk=n_xblk),
        grid=(B, n_xblk + 1),
        in_specs=[
            pl.BlockSpec((1, BLK, D), lambda b, i: (b, jnp.minimum(i, n_xblk - 1), 0)),
            pl.BlockSpec((BLK, D), lambda b, i: (0, 0)),
            pl.BlockSpec((1, D), lambda b, i: (0, 0)),
        ],
        out_specs=pl.BlockSpec((1, BLK, D), lambda b, i: (b, i, 0)),
        out_shape=jax.ShapeDtypeStruct((B, S + BLK, D), BF16),
        compiler_params=_params("parallel", "arbitrary"),
        name="prenorm",
    )(x, meta_blk, gain)


def _proj_kernel(h_ref, w_ref, *rest, epilogue):
    *aux, o_ref = rest
    acc = _dot(h_ref[0], w_ref[...])
    epilogue(acc, aux, o_ref)


def _proj(hn, w, aux, epilogue, *, rows, tm, tn, out_dtype, name):
    B, _, D = hn.shape
    N = w.shape[1]
    tn = min(tn, N)
    assert N % tn == 0 and rows % tm == 0
    assert all(a.shape == (1, N) for a in aux)
    aux_specs = [pl.BlockSpec((1, tn), lambda b, i, j: (0, j)) for a in aux]
    return pl.pallas_call(
        functools.partial(_proj_kernel, epilogue=epilogue),
        grid=(B, rows // tm, N // tn),
        in_specs=[
            pl.BlockSpec((1, tm, D), lambda b, i, j: (b, i, 0)),
            pl.BlockSpec((D, tn), lambda b, i, j: (0, j)),
        ] + aux_specs,
        out_specs=pl.BlockSpec((1, tm, tn), lambda b, i, j: (b, i, j)),
        out_shape=jax.ShapeDtypeStruct((B, rows, N), out_dtype),
        compiler_params=_params("parallel", "parallel", "arbitrary"),
        name=name,
    )(hn, w, *aux)


def _epi_copy(acc, aux, o_ref):
    o_ref[0] = acc.astype(o_ref.dtype)


def _epi_silu(acc, aux, o_ref):
    o_ref[0] = (acc * _sigmoid(acc)).astype(o_ref.dtype)


def _epi_sigmoid(acc, aux, o_ref):
    o_ref[0] = _sigmoid(acc).astype(o_ref.dtype)


def _epi_decay_beta(acc, aux, o_ref, *, n_heads, seq, tm):
    alog_ref, dtb_ref = aux
    i = pl.program_id(1)
    col = lax.broadcasted_iota(jnp.int32, acc.shape, 1)
    row = i * tm + lax.broadcasted_iota(jnp.int32, acc.shape, 0)
    g = -jnp.exp(alog_ref[...]) * _softplus(acc + dtb_ref[...])
    out = jnp.where(col < n_heads, g, _sigmoid(acc))
    is_pad = (row >= seq) & (row < seq + N_PAD)
    o_ref[0] = jnp.where(is_pad, 0.0, out)


def _epi_sb_qkv(acc, aux, o_ref, *, dh):
    (gain_ref,) = aux
    j = pl.program_id(2)

    @pl.when(j < 2)
    def _():
        for c in range(0, acc.shape[1], dh):
            y = acc[:, c:c + dh]
            ms = jnp.mean(y * y, axis=-1, keepdims=True)
            o_ref[0, :, c:c + dh] = (y * lax.rsqrt(ms + RMS_EPS) * gain_ref[:, c:c + dh]).astype(o_ref.dtype)

    @pl.when(j == 2)
    def _():
        o_ref[0] = acc.astype(o_ref.dtype)


def _inv_unit_lower(L, ixj):
    n = L.shape[0]
    eye = (ixj == 0).astype(F32)
    X = eye - jnp.where(ixj == 1, L, 0.0)
    h = 2
    while h < n:
        Cl = jnp.where((ixj >= h) & (ixj < 2 * h), L, 0.0)
        X = X - _dot_split(_dot_split(X, Cl, INV_PASSES), X, INV_PASSES)
        h *= 2
    return X


def _gdn_kernel(x_ref, cw_ref, gb_ref, grow_ref, dz_ref, gain_ref, o_ref, xbuf, s_ref,
                *, n_heads, dk, dv):
    c = pl.program_id(1)
    C = CHUNK
    qk_w = n_heads * dk

    @pl.when(c == 0)
    def _():
        xbuf[0:8, :] = jnp.zeros((8, xbuf.shape[1]), F32)
        s_ref[...] = jnp.zeros(s_ref.shape, F32)

    xbuf[8:8 + C, :] = x_ref[0]

    def conv_silu(c0, width):
        y = cw_ref[3:4, c0:c0 + width] * xbuf[8:8 + C, c0:c0 + width]
        for tap in range(3):
            y = y + cw_ref[tap:tap + 1, c0:c0 + width] * xbuf[5 + tap:5 + tap + C, c0:c0 + width]
        return y * _sigmoid(y)

    def l2norm(a):
        return a * lax.rsqrt(jnp.sum(a * a, axis=-1, keepdims=True) + L2_EPS)

    ri = lax.broadcasted_iota(jnp.int32, (C, C), 0)
    ci = lax.broadcasted_iota(jnp.int32, (C, C), 1)
    ixj = ri ^ ci
    incl = ri >= ci
    strict = ri > ci

    gb = gb_ref[0]
    tri = incl.astype(BF16)
    tri_t = (ri <= ci).astype(BF16)
    gcum_c = sum(_dot(tri, p) for p in _split(gb, 3))
    gcum_r = sum(_dot(p, tri_t) for p in _split(grow_ref[0, 0], 3))

    for h in range(n_heads):
        q = l2norm(conv_silu(h * dk, dk)) * (dk ** -0.5)
        k = l2norm(conv_silu(qk_w + h * dk, dk))
        v = conv_silu(2 * qk_w + h * dv, dv)
        beta = gb[:, n_heads + h:n_heads + h + 1]
        gc = gcum_c[:, h:h + 1]
        gr = gcum_r[h:h + 1, :]
        g_last = gcum_c[C - 1:C, h:h + 1]

        decay = jnp.exp(jnp.where(incl, gc - gr, NEG_BIG))
        kb = k * beta
        kq = _dot(jnp.concatenate([kb, q], axis=0).astype(BF16), k.astype(BF16), _NT)
        lower = jnp.where(strict, kq[:C] * decay, 0.0)
        attn = kq[C:] * decay

        X = _inv_unit_lower(lower, ixj)
        e_g = jnp.exp(gc)
        rhs = jnp.concatenate([v * beta, kb * e_g], axis=1)
        sol = _dot_split(X, rhs, INV_PASSES)
        u, w = sol[:, :dv], sol[:, dv:]

        S = s_ref[h]
        wq = jnp.concatenate([w, q * e_g], axis=0).astype(BF16)
        ws_qs = _dot(wq, S.astype(BF16))
        v_new = u - ws_qs[:C]
        v_new_b = v_new.astype(BF16)
        o = ws_qs[C:] + _dot(attn.astype(BF16), v_new_b)
        k_dec = (k * jnp.exp(g_last - gc)).astype(BF16)
        s_ref[h] = S * jnp.exp(g_last) + _dot(k_dec, v_new_b, _TN)

        @pl.when(c > 0)
        def _():
            ms = jnp.mean(o * o, axis=-1, keepdims=True)
            on = o * lax.rsqrt(ms + RMS_EPS) * gain_ref[...]
            o_ref[0, :, h * dv:(h + 1) * dv] = (on * dz_ref[0, :, h * dv:(h + 1) * dv]).astype(o_ref.dtype)

    xbuf[0:8, :] = xbuf[C:C + 8, :]


def _gdn(dqkv, conv_w, gb, grow, dz, gain, *, seq, n_heads, dk, dv):
    B = dqkv.shape[0]
    W = dqkv.shape[2]
    n_chunks = seq // CHUNK + 1
    meta_chunk = seq // CHUNK + 1

    def tl(b, c):
        return (b, jnp.where(c == 0, meta_chunk, c - 1), 0)

    def xr(b, c):
        return (b, jnp.maximum(c - 1, 0), 0)

    return pl.pallas_call(
        functools.partial(_gdn_kernel, n_heads=n_heads, dk=dk, dv=dv),
        grid=(B, n_chunks),
        in_specs=[
            pl.BlockSpec((1, CHUNK, W), tl),
            pl.BlockSpec((4, W), lambda b, c: (0, 0)),
            pl.BlockSpec((1, CHUNK, gb.shape[2]), tl),
            pl.BlockSpec((1, 1, n_heads, CHUNK), lambda b, c: (b, c, 0, 0)),
            pl.BlockSpec((1, CHUNK, n_heads * dv), xr),
            pl.BlockSpec((1, dv), lambda b, c: (0, 0)),
        ],
        out_specs=pl.BlockSpec((1, CHUNK, n_heads * dv), xr),
        out_shape=jax.ShapeDtypeStruct((B, seq, n_heads * dv), BF16),
        scratch_shapes=[
            pltpu.VMEM((CHUNK + 8, W), F32),
            pltpu.VMEM((n_heads, dk, dv), F32),
        ],
        compiler_params=_params("parallel", "arbitrary"),
        name="gdn",
    )(dqkv, conv_w, gb, grow, dz, gain)


def _sb_kernel(q_ref, k_ref, v_ref, o_ref, *, n_xblk):
    i = pl.program_id(2)
    q = q_ref[0]
    ri = lax.broadcasted_iota(jnp.int32, (BLK, BLK), 0)
    ci = lax.broadcasted_iota(jnp.int32, (BLK, BLK), 1)
    q_pos = (i + 1) * BLK + ri
    r2 = lax.broadcasted_iota(jnp.int32, (BLK, 2 * BLK), 0)
    c2 = lax.broadcasted_iota(jnp.int32, (BLK, 2 * BLK), 1)
    tail = ((r2 > c2) | (c2 >= BLK)).astype(BF16)

    def body(t, carry):
        acc, later = carry
        jb = i + 1 - t
        start = pl.multiple_of(jnp.where(jb == 0, n_xblk * BLK, (jb - 1) * BLK), BLK)
        kj = k_ref[0, pl.ds(start, BLK), :]
        vj = v_ref[0, pl.ds(start, BLK), :]
        z = _dot(q, kj, _NT)
        k_pos = jb * BLK + ci
        vis = (k_pos < q_pos) & (k_pos >= N_PAD)
        sp = _softplus(z)
        spm = jnp.where(vis, sp, 0.0)
        hi, lo = _split(spm, 2)
        sums = _dot(hi, tail) + _dot(lo, tail)
        w = jnp.where(vis, jnp.exp(z - sp - (later + sums[:, :BLK])), 0.0)
        acc = acc + _dot(w.astype(BF16), vj)
        return acc, later + sums[:, BLK:]

    zero = jnp.zeros((BLK, BLK), F32)
    acc, _ = lax.fori_loop(0, i + 2, body, (zero, zero))
    o_ref[0] = acc.astype(o_ref.dtype)


def _stick_breaking(qkv, *, seq, n_heads):
    B, rows, _ = qkv.shape
    n_xblk = seq // BLK
    return pl.pallas_call(
        functools.partial(_sb_kernel, n_xblk=n_xblk),
        grid=(B, n_heads, n_xblk),
        in_specs=[
            pl.BlockSpec((1, BLK, BLK), lambda b, h, i: (b, i, h)),
            pl.BlockSpec((1, rows, BLK), lambda b, h, i: (b, 0, n_heads + h)),
            pl.BlockSpec((1, rows, BLK), lambda b, h, i: (b, 0, 2 * n_heads + h)),
        ],
        out_specs=pl.BlockSpec((1, BLK, BLK), lambda b, h, i: (b, i, h)),
        out_shape=jax.ShapeDtypeStruct((B, seq, n_heads * BLK), BF16),
        compiler_params=_params("parallel", "parallel", "arbitrary"),
        name="stick_breaking",
    )(qkv, qkv, qkv)


def _merge_kernel(x_ref, odn_ref, osb_ref, gate_ref, wdn_ref, wsb_ref, wo_ref, o_ref):
    D = x_ref.shape[2]
    a = _dot(odn_ref[0], wdn_ref[...])
    b = _dot(osb_ref[0], wsb_ref[...])
    merged = gate_ref[0, :, :D] * a + gate_ref[0, :, D:] * b
    o_ref[0] = x_ref[0] + _dot(merged.astype(BF16), wo_ref[...])


def _merge(x, o_dn, o_sb, gates, w_dn, w_sb, w_o, *, tm):
    B, S, D = x.shape
    full = lambda a: pl.BlockSpec(a.shape, lambda b, i: (0,) * a.ndim)
    rows = lambda a: pl.BlockSpec((1, tm, a.shape[2]), lambda b, i: (b, i, 0))
    return pl.pallas_call(
        _merge_kernel,
        grid=(B, S // tm),
        in_specs=[rows(x), rows(o_dn), rows(o_sb), rows(gates), full(w_dn), full(w_sb), full(w_o)],
        out_specs=rows(x),
        out_shape=jax.ShapeDtypeStruct((B, S, D), F32),
        compiler_params=_params("parallel", "parallel"),
        name="merge",
    )(x, o_dn, o_sb, gates, w_dn, w_sb, w_o)


def _ffn_kernel(h_ref, gain_ref, wg_ref, wu_ref, wo_ref, o_ref):
    h = h_ref[0]
    ms = jnp.mean(h * h, axis=-1, keepdims=True)
    hn = (h * lax.rsqrt(ms + RMS_EPS) * gain_ref[...]).astype(BF16)
    o_ref[0] = h
    for c in range(wg_ref.shape[0]):
        g = _dot(hn, wg_ref[c])
        u = _dot(hn, wu_ref[c])
        act = (g * _sigmoid(g) * u).astype(BF16)
        o_ref[0] += _dot(act, wo_ref[c])


def _ffn(h, gain, wg, wu, wo, *, tm):
    B, S, D = h.shape
    full = lambda a: pl.BlockSpec(a.shape, lambda b, i: (0,) * a.ndim)
    rows = pl.BlockSpec((1, tm, D), lambda b, i: (b, i, 0))
    return pl.pallas_call(
        _ffn_kernel,
        grid=(B, S // tm),
        in_specs=[rows, full(gain), full(wg), full(wu), full(wo)],
        out_specs=rows,
        out_shape=jax.ShapeDtypeStruct((B, S, D), F32),
        compiler_params=_params("parallel", "parallel"),
        name="ffn",
    )(h, gain, wg, wu, wo)


def kernel(x, meta_tokens, norm_mix_gain, w_in, conv_q, conv_k, conv_v, dn_a_log, dn_dt_bias,
           dn_out_norm_gain, sb_q_norm_gain, sb_k_norm_gain, w_branch_dn, w_branch_sb, w_out,
           norm_ffn_gain, w_ffn_in, w_ffn_out):
    B, S, D = x.shape
    assert norm_mix_gain.shape[0] == 1, "one layer: the meta rows of the stream are not carried to a next layer"
    assert meta_tokens.shape[0] == N_META and S % BLK == 0
    H = dn_a_log.shape[-1]
    qk_w, v_w = conv_q.shape[-1], conv_v.shape[-1]
    dk, dv = qk_w // H, v_w // H
    dh = sb_q_norm_gain.shape[-1]
    sb_w = w_branch_sb.shape[-2]
    sb_heads = sb_w // dh
    d_ff = w_ffn_out.shape[-2]
    assert dk % 128 == 0 and dv % 128 == 0 and dh == BLK and 2 * H <= 128 and d_ff % FF_CHUNK == 0
    rows = S + BLK

    wi = w_in[0]
    o_z = 2 * qk_w + v_w
    o_a = o_z + v_w
    o_s = o_a + 2 * H
    o_g = o_s + 3 * sb_w
    w_dqkv = wi[:, :o_z].astype(BF16)
    w_dz = wi[:, o_z:o_a].astype(BF16)
    w_ab = jnp.pad(wi[:, o_a:o_s], ((0, 0), (0, 128 - 2 * H))).astype(BF16)
    w_sb = wi[:, o_s:o_g].astype(BF16)
    w_gate = wi[:, o_g:].astype(BF16)
    lane_pad = lambda a: jnp.pad(a.astype(F32), (0, 128 - a.shape[0]))[None, :]
    a_log = lane_pad(dn_a_log[0])
    dt_bias = lane_pad(dn_dt_bias[0])
    sb_gain = jnp.concatenate([jnp.tile(sb_q_norm_gain[0] * (dh ** -0.5), sb_heads),
                               jnp.tile(sb_k_norm_gain[0], sb_heads),
                               jnp.ones((sb_w,), F32)])[None, :].astype(F32)
    conv_w = jnp.concatenate([conv_q[0], conv_k[0], conv_v[0]], axis=1).astype(F32)
    meta_blk = jnp.pad(meta_tokens.astype(F32), ((N_PAD, 0), (0, 0)))
    n_ffc = d_ff // FF_CHUNK
    wfi = w_ffn_in[0].astype(BF16)
    wg = wfi[:, :d_ff].reshape(D, n_ffc, FF_CHUNK).transpose(1, 0, 2)
    wu = wfi[:, d_ff:].reshape(D, n_ffc, FF_CHUNK).transpose(1, 0, 2)
    wo = w_ffn_out[0].astype(BF16).reshape(n_ffc, FF_CHUNK, D)

    hn = _prenorm(x, meta_blk, norm_mix_gain[0][None, :].astype(F32))

    tm_all = _row_tile(rows)
    tm_x = _row_tile(S, 1024)
    dqkv = _proj(hn, w_dqkv, [], _epi_copy, rows=rows, tm=tm_all, tn=1024, out_dtype=F32, name="proj_dqkv")
    dz = _proj(hn, w_dz, [], _epi_silu, rows=S, tm=tm_x, tn=1024, out_dtype=F32, name="proj_dz")
    gb = _proj(hn, w_ab, [a_log, dt_bias],
               functools.partial(_epi_decay_beta, n_heads=H, seq=S, tm=tm_all),
               rows=rows, tm=tm_all, tn=128, out_dtype=F32, name="proj_decay_beta")
    sb_qkv = _proj(hn, w_sb, [sb_gain], functools.partial(_epi_sb_qkv, dh=dh),
                   rows=rows, tm=tm_all, tn=sb_w, out_dtype=BF16, name="proj_sb_qkv")
    gates = _proj(hn, w_gate, [], _epi_sigmoid, rows=S, tm=tm_x, tn=1024, out_dtype=F32, name="proj_gates")

    g_rows = jnp.concatenate([gb[:, S + BLK - CHUNK:, :H], gb[:, :S, :H]], axis=1)
    grow = g_rows.reshape(B, S // CHUNK + 1, CHUNK, H).transpose(0, 1, 3, 2)

    o_dn = _gdn(dqkv, conv_w, gb, grow, dz, dn_out_norm_gain[0][None, :].astype(F32),
                seq=S, n_heads=H, dk=dk, dv=dv)
    o_sb = _stick_breaking(sb_qkv, seq=S, n_heads=sb_heads)

    h1 = _merge(x, o_dn, o_sb, gates, w_branch_dn[0].astype(BF16), w_branch_sb[0].astype(BF16),
                w_out[0].astype(BF16), tm=_row_tile(S, 512))
    return _ffn(h1, norm_ffn_gain[0][None, :].astype(F32), wg, wu, wo, tm=_row_tile(S, 512))
```

```python
import functools

import jax
import jax.numpy as jnp
from jax import lax
from jax.experimental import pallas as pl
from jax.experimental.pallas import tpu as pltpu

F32 = jnp.float32
BF16 = jnp.bfloat16

N_META = 16
CHUNK = 64
BLK = 128
N_PAD = BLK - N_META
RMS_EPS = 1e-6
L2_EPS = 1e-6
NEG_BIG = -1e30
FF_CHUNK = 256
INV_PASSES = 3
SB_GROUP = 8
SB_EXIT_SUM = 105.0
VMEM_LIMIT = 56 * 1024 * 1024

_NT = (((1,), (1,)), ((), ()))
_TN = (((0,), (0,)), ((), ()))


def _params(*sem):
    return pltpu.CompilerParams(dimension_semantics=sem, vmem_limit_bytes=VMEM_LIMIT)


def _dot(a, b, dims=None):
    if dims is None:
        return jnp.dot(a, b, preferred_element_type=F32)
    return lax.dot_general(a, b, dims, preferred_element_type=F32)


def _split(x, n):
    parts = []
    for _ in range(n - 1):
        p = x.astype(BF16)
        parts.append(p)
        x = x - p.astype(F32)
    parts.append(x.astype(BF16))
    return parts


def _dot_split(a, b, passes):
    if passes == 1:
        return _dot(a.astype(BF16), b.astype(BF16))
    ah, al = _split(a, 2)
    bh, bl = _split(b, 2)
    out = _dot(ah, bh) + _dot(al, bh)
    if passes >= 3:
        out = out + _dot(ah, bl)
    return out


def _sigmoid(x):
    return 1.0 / (1.0 + jnp.exp(-x))


def _softplus(x):
    return jnp.maximum(x, 0.0) + jnp.log(1.0 + jnp.exp(-jnp.abs(x)))


def _row_tile(n, cap=1152):
    best = 16
    for t in range(16, min(n, cap) + 1, 16):
        if n % t == 0:
            best = t
    return best


def _prenorm_kernel(x_ref, meta_ref, gain_ref, o_ref, *, n_xblk):
    i = pl.program_id(1)
    gain = gain_ref[...]

    def norm(v):
        ms = jnp.mean(v * v, axis=-1, keepdims=True)
        return (v * lax.rsqrt(ms + RMS_EPS) * gain).astype(o_ref.dtype)

    @pl.when(i < n_xblk)
    def _():
        o_ref[0] = norm(x_ref[0])

    @pl.when(i == n_xblk)
    def _():
        o_ref[0] = norm(meta_ref[...])


def _prenorm(x, meta_blk, gain):
    B, S, D = x.shape
    n_xblk = S // BLK
    return pl.pallas_call(
        functools.partial(_prenorm_kernel, n_xblk=n_xblk),
        grid=(B, n_xblk + 1),
        in_specs=[
            pl.BlockSpec((1, BLK, D), lambda b, i: (b, jnp.minimum(i, n_xblk - 1), 0)),
            pl.BlockSpec((BLK, D), lambda b, i: (0, 0)),
            pl.BlockSpec((1, D), lambda b, i: (0, 0)),
        ],
        out_specs=pl.BlockSpec((1, BLK, D), lambda b, i: (b, i, 0)),
        out_shape=jax.ShapeDtypeStruct((B, S + BLK, D), BF16),
        compiler_params=_params("parallel", "arbitrary"),
        name="prenorm",
    )(x, meta_blk, gain)


def _proj_kernel(h_ref, w_ref, *rest, epilogue):
    *aux, o_ref = rest
    acc = _dot(h_ref[0], w_ref[...])
    epilogue(acc, aux, o_ref)


def _proj(hn, w, aux, epilogue, *, rows, tm, tn, out_dtype, name):
    B, _, D = hn.shape
    N = w.shape[1]
    tn = min(tn, N)
    assert N % tn == 0 and rows % tm == 0
    assert all(a.shape == (1, N) for a in aux)
    aux_specs = [pl.BlockSpec((1, tn), lambda b, i, j: (0, j)) for a in aux]
    return pl.pallas_call(
        functools.partial(_proj_kernel, epilogue=epilogue),
        grid=(B, rows // tm, N // tn),
        in_specs=[
            pl.BlockSpec((1, tm, D), lambda b, i, j: (b, i, 0)),
            pl.BlockSpec((D, tn), lambda b, i, j: (0, j)),
        ] + aux_specs,
        out_specs=pl.BlockSpec((1, tm, tn), lambda b, i, j: (b, i, j)),
        out_shape=jax.ShapeDtypeStruct((B, rows, N), out_dtype),
        compiler_params=_params("parallel", "parallel", "arbitrary"),
        name=name,
    )(hn, w, *aux)


def _epi_copy(acc, aux, o_ref):
    o_ref[0] = acc.astype(o_ref.dtype)


def _epi_silu(acc, aux, o_ref):
    o_ref[0] = (acc * _sigmoid(acc)).astype(o_ref.dtype)


def _epi_sigmoid(acc, aux, o_ref):
    o_ref[0] = _sigmoid(acc).astype(o_ref.dtype)


def _epi_decay_beta(acc, aux, o_ref, *, n_heads, seq, tm):
    alog_ref, dtb_ref = aux
    i = pl.program_id(1)
    col = lax.broadcasted_iota(jnp.int32, acc.shape, 1)
    row = i * tm + lax.broadcasted_iota(jnp.int32, acc.shape, 0)
    g = -jnp.exp(alog_ref[...]) * _softplus(acc + dtb_ref[...])
    out = jnp.where(col < n_heads, g, _sigmoid(acc))
    is_pad = (row >= seq) & (row < seq + N_PAD)
    o_ref[0] = jnp.where(is_pad, 0.0, out)


def _epi_sb_qkv(acc, aux, o_ref, *, dh):
    (gain_ref,) = aux
    j = pl.program_id(2)

    @pl.when(j < 2)
    def _():
        for c in range(0, acc.shape[1], dh):
            y = acc[:, c:c + dh]
            ms = jnp.mean(y * y, axis=-1, keepdims=True)
            o_ref[0, :, c:c + dh] = (y * lax.rsqrt(ms + RMS_EPS) * gain_ref[:, c:c + dh]).astype(o_ref.dtype)

    @pl.when(j == 2)
    def _():
        o_ref[0] = acc.astype(o_ref.dtype)


def _inv_unit_lower(Ls, ixj):
    n = Ls[0].shape[0]
    eye = (ixj == 0).astype(F32)
    Xs = [eye - jnp.where(ixj == 1, L, 0.0) for L in Ls]
    h = 2
    while h < n:
        join = (ixj >= h) & (ixj < 2 * h)
        XC = [_dot_split(X, jnp.where(join, L, 0.0), INV_PASSES) for X, L in zip(Xs, Ls)]
        Xs = [X - _dot_split(xc, X, INV_PASSES) for X, xc in zip(Xs, XC)]
        h *= 2
    return Xs


def _gdn_kernel(x_ref, cw_ref, gb_ref, grow_ref, dz_ref, gain_ref, o_ref, xbuf, s_ref,
                *, n_heads, dk, dv):
    c = pl.program_id(1)
    C = CHUNK
    qk_w = n_heads * dk

    @pl.when(c == 0)
    def _():
        xbuf[0:8, :] = jnp.zeros((8, xbuf.shape[1]), F32)
        s_ref[...] = jnp.zeros(s_ref.shape, F32)

    xbuf[8:8 + C, :] = x_ref[0]

    def conv_silu(c0, width):
        y = cw_ref[3:4, c0:c0 + width] * xbuf[8:8 + C, c0:c0 + width]
        for tap in range(3):
            y = y + cw_ref[tap:tap + 1, c0:c0 + width] * xbuf[5 + tap:5 + tap + C, c0:c0 + width]
        return y * _sigmoid(y)

    def l2norm(a):
        return a * lax.rsqrt(jnp.sum(a * a, axis=-1, keepdims=True) + L2_EPS)

    ri = lax.broadcasted_iota(jnp.int32, (C, C), 0)
    ci = lax.broadcasted_iota(jnp.int32, (C, C), 1)
    ixj = ri ^ ci
    incl = ri >= ci
    strict = ri > ci

    gb = gb_ref[0]
    tri = incl.astype(BF16)
    tri_t = (ri <= ci).astype(BF16)
    gcum_c = sum(_dot(tri, p) for p in _split(gb, 3))
    gcum_r = sum(_dot(p, tri_t) for p in _split(grow_ref[0, 0], 3))

    heads = range(n_heads)
    q = [l2norm(conv_silu(h * dk, dk)) * (dk ** -0.5) for h in heads]
    k = [l2norm(conv_silu(qk_w + h * dk, dk)) for h in heads]
    v = [conv_silu(2 * qk_w + h * dv, dv) for h in heads]
    beta = [gb[:, n_heads + h:n_heads + h + 1] for h in heads]
    gc = [gcum_c[:, h:h + 1] for h in heads]
    g_last = [gcum_c[C - 1:C, h:h + 1] for h in heads]
    decay = [jnp.exp(jnp.where(incl, gc[h] - gcum_r[h:h + 1, :], NEG_BIG)) for h in heads]
    kb = [k[h] * beta[h] for h in heads]
    kq = [_dot(jnp.concatenate([kb[h], q[h]], axis=0).astype(BF16), k[h].astype(BF16), _NT) for h in heads]
    lower = [jnp.where(strict, kq[h][:C] * decay[h], 0.0) for h in heads]
    attn = [(kq[h][C:] * decay[h]).astype(BF16) for h in heads]

    X = _inv_unit_lower(lower, ixj)
    e_g = [jnp.exp(gc[h]) for h in heads]
    sol = [_dot_split(X[h], jnp.concatenate([v[h] * beta[h], kb[h] * e_g[h]], axis=1), INV_PASSES)
           for h in heads]

    S = [s_ref[h] for h in heads]
    ws_qs = [_dot(jnp.concatenate([sol[h][:, dv:], q[h] * e_g[h]], axis=0).astype(BF16), S[h].astype(BF16))
             for h in heads]
    v_new = [(sol[h][:, :dv] - ws_qs[h][:C]).astype(BF16) for h in heads]
    o = [ws_qs[h][C:] + _dot(attn[h], v_new[h]) for h in heads]
    for h in heads:
        k_dec = (k[h] * jnp.exp(g_last[h] - gc[h])).astype(BF16)
        s_ref[h] = S[h] * jnp.exp(g_last[h]) + _dot(k_dec, v_new[h], _TN)

    for h in heads:
        ms = jnp.mean(o[h] * o[h], axis=-1, keepdims=True)
        on = o[h] * lax.rsqrt(ms + RMS_EPS) * gain_ref[...]
        o_ref[0, :, h * dv:(h + 1) * dv] = (on * dz_ref[0, :, h * dv:(h + 1) * dv]).astype(o_ref.dtype)

    xbuf[0:8, :] = xbuf[C:C + 8, :]


def _gdn(dqkv, conv_w, gb, grow, dz, gain, *, seq, n_heads, dk, dv):
    B = dqkv.shape[0]
    W = dqkv.shape[2]
    n_chunks = seq // CHUNK + 1
    meta_chunk = seq // CHUNK + 1

    def tl(b, c):
        return (b, jnp.where(c == 0, meta_chunk, c - 1), 0)

    def xr(b, c):
        return (b, jnp.maximum(c - 1, 0), 0)

    return pl.pallas_call(
        functools.partial(_gdn_kernel, n_heads=n_heads, dk=dk, dv=dv),
        grid=(B, n_chunks),
        in_specs=[
            pl.BlockSpec((1, CHUNK, W), tl),
            pl.BlockSpec((4, W), lambda b, c: (0, 0)),
            pl.BlockSpec((1, CHUNK, gb.shape[2]), tl),
            pl.BlockSpec((1, 1, n_heads, CHUNK), lambda b, c: (b, c, 0, 0)),
            pl.BlockSpec((1, CHUNK, n_heads * dv), xr),
            pl.BlockSpec((1, dv), lambda b, c: (0, 0)),
        ],
        out_specs=pl.BlockSpec((1, CHUNK, n_heads * dv), xr),
        out_shape=jax.ShapeDtypeStruct((B, seq, n_heads * dv), BF16),
        scratch_shapes=[
            pltpu.VMEM((CHUNK + 8, W), F32),
            pltpu.VMEM((n_heads, dk, dv), F32),
        ],
        compiler_params=_params("parallel", "arbitrary"),
        name="gdn",
    )(dqkv, conv_w, gb, grow, dz, gain)


def _sb_kernel(q_ref, k_ref, v_ref, o_ref, acc_ref, later_ref, *, n_xblk, group):
    i0 = pl.program_id(2) * group
    chains = range(group)
    ri = lax.broadcasted_iota(jnp.int32, (BLK, BLK), 0)
    ci = lax.broadcasted_iota(jnp.int32, (BLK, BLK), 1)
    diag_bias = jnp.where(ci < ri, 0.0, NEG_BIG)
    lane = lax.broadcasted_iota(jnp.int32, (1, BLK), 1)
    meta_bias = jnp.where(lane >= N_PAD, 0.0, NEG_BIG)
    r2 = lax.broadcasted_iota(jnp.int32, (2 * BLK, 2 * BLK), 0) & (BLK - 1)
    c2 = lax.broadcasted_iota(jnp.int32, (2 * BLK, 2 * BLK), 1)
    tail2 = ((r2 > c2) | (c2 >= BLK)).astype(BF16)

    def trip(jbs, biases, first):
        starts = [pl.multiple_of(jnp.where(jb <= 0, n_xblk * BLK, (jb - 1) * BLK), BLK) for jb in jbs]
        z = [_dot(q_ref[0, c * BLK:(c + 1) * BLK, :], k_ref[0, pl.ds(starts[c], BLK), :], _NT) + biases[c]
             for c in chains]
        sp = [_softplus(z[c]) for c in chains]
        sums = [_dot(jnp.concatenate(_split(sp[c], 2), axis=1), tail2) for c in chains]
        laters = []
        for c in chains:
            if first:
                later = sums[c][:, BLK:]
                acc_ref[c] = _dot(jnp.exp(z[c] - sp[c] - sums[c][:, :BLK]).astype(BF16),
                                  v_ref[0, pl.ds(starts[c], BLK), :])
            else:
                prev = later_ref[c]
                later = prev + sums[c][:, BLK:]
                acc_ref[c] += _dot(jnp.exp(z[c] - sp[c] - (prev + sums[c][:, :BLK])).astype(BF16),
                                   v_ref[0, pl.ds(starts[c], BLK), :])
            later_ref[c] = later
            laters.append(later)
        return laters

    def pending(laters, t_next):
        m = jnp.full((8, BLK), -NEG_BIG, F32)
        for c, later in enumerate(laters):
            col = jnp.min(later.reshape(BLK // 8, 8, BLK), axis=0)
            m = jnp.minimum(m, jnp.where(i0 + c + 1 - t_next >= 0, col, -NEG_BIG))
        return jnp.min(m)

    laters = trip([i0 + c + 1 for c in chains], [diag_bias] * group, True)

    def cond(carry):
        t, low = carry
        return low <= SB_EXIT_SUM

    def body(carry):
        t, _ = carry
        jbs = [i0 + c + 1 - t for c in chains]
        biases = [jnp.where(jb >= 1, 0.0, jnp.where(jb == 0, meta_bias, NEG_BIG)) for jb in jbs]
        return t + 1, pending(trip(jbs, biases, False), t + 1)

    lax.while_loop(cond, body, (jnp.int32(1), pending(laters, 1)))
    for c in range(group):
        o_ref[0, c * BLK:(c + 1) * BLK, :] = acc_ref[c].astype(o_ref.dtype)


def _stick_breaking(qkv, *, seq, n_heads):
    B, rows, _ = qkv.shape
    n_xblk = seq // BLK
    group = max(g for g in range(1, SB_GROUP + 1) if n_xblk % g == 0)
    return pl.pallas_call(
        functools.partial(_sb_kernel, n_xblk=n_xblk, group=group),
        grid=(B, n_heads, n_xblk // group),
        in_specs=[
            pl.BlockSpec((1, group * BLK, BLK), lambda b, h, g: (b, g, h)),
            pl.BlockSpec((1, rows, BLK), lambda b, h, g: (b, 0, n_heads + h)),
            pl.BlockSpec((1, rows, BLK), lambda b, h, g: (b, 0, 2 * n_heads + h)),
        ],
        out_specs=pl.BlockSpec((1, group * BLK, BLK), lambda b, h, g: (b, g, h)),
        out_shape=jax.ShapeDtypeStruct((B, seq, n_heads * BLK), BF16),
        scratch_shapes=[
            pltpu.VMEM((group, BLK, BLK), F32),
            pltpu.VMEM((group, BLK, BLK), F32),
        ],
        compiler_params=_params("parallel", "parallel", "arbitrary"),
        name="stick_breaking",
    )(qkv, qkv, qkv)


def _merge_kernel(x_ref, odn_ref, osb_ref, gate_ref, wdn_ref, wsb_ref, wo_ref, o_ref):
    D = x_ref.shape[2]
    a = _dot(odn_ref[0], wdn_ref[...])
    b = _dot(osb_ref[0], wsb_ref[...])
    merged = gate_ref[0, :, :D] * a + gate_ref[0, :, D:] * b
    o_ref[0] = x_ref[0] + _dot(merged.astype(BF16), wo_ref[...])


def _merge(x, o_dn, o_sb, gates, w_dn, w_sb, w_o, *, tm):
    B, S, D = x.shape
    full = lambda a: pl.BlockSpec(a.shape, lambda b, i: (0,) * a.ndim)
    rows = lambda a: pl.BlockSpec((1, tm, a.shape[2]), lambda b, i: (b, i, 0))
    return pl.pallas_call(
        _merge_kernel,
        grid=(B, S // tm),
        in_specs=[rows(x), rows(o_dn), rows(o_sb), rows(gates), full(w_dn), full(w_sb), full(w_o)],
        out_specs=rows(x),
        out_shape=jax.ShapeDtypeStruct((B, S, D), F32),
        compiler_params=_params("parallel", "parallel"),
        name="merge",
    )(x, o_dn, o_sb, gates, w_dn, w_sb, w_o)


def _ffn_kernel(h_ref, gain_ref, wg_ref, wu_ref, wo_ref, o_ref):
    h = h_ref[0]
    ms = jnp.mean(h * h, axis=-1, keepdims=True)
    hn = (h * lax.rsqrt(ms + RMS_EPS) * gain_ref[...]).astype(BF16)
    o_ref[0] = h
    for c in range(wg_ref.shape[0]):
        g = _dot(hn, wg_ref[c])
        u = _dot(hn, wu_ref[c])
        act = (g * _sigmoid(g) * u).astype(BF16)
        o_ref[0] += _dot(act, wo_ref[c])


def _ffn(h, gain, wg, wu, wo, *, tm):
    B, S, D = h.shape
    full = lambda a: pl.BlockSpec(a.shape, lambda b, i: (0,) * a.ndim)
    rows = pl.BlockSpec((1, tm, D), lambda b, i: (b, i, 0))
    return pl.pallas_call(
        _ffn_kernel,
        grid=(B, S // tm),
        in_specs=[rows, full(gain), full(wg), full(wu), full(wo)],
        out_specs=rows,
        out_shape=jax.ShapeDtypeStruct((B, S, D), F32),
        compiler_params=_params("parallel", "parallel"),
        name="ffn",
    )(h, gain, wg, wu, wo)


def kernel(x, meta_tokens, norm_mix_gain, w_in, conv_q, conv_k, conv_v, dn_a_log, dn_dt_bias,
           dn_out_norm_gain, sb_q_norm_gain, sb_k_norm_gain, w_branch_dn, w_branch_sb, w_out,
           norm_ffn_gain, w_ffn_in, w_ffn_out):
    B, S, D = x.shape
    assert norm_mix_gain.shape[0] == 1, "one layer: the meta rows of the stream are not carried to a next layer"
    assert meta_tokens.shape[0] == N_META and S % BLK == 0
    H = dn_a_log.shape[-1]
    qk_w, v_w = conv_q.shape[-1], conv_v.shape[-1]
    dk, dv = qk_w // H, v_w // H
    dh = sb_q_norm_gain.shape[-1]
    sb_w = w_branch_sb.shape[-2]
    sb_heads = sb_w // dh
    d_ff = w_ffn_out.shape[-2]
    assert dk % 128 == 0 and dv % 128 == 0 and dh == BLK and 2 * H <= 128 and d_ff % FF_CHUNK == 0
    rows = S + BLK

    wi = w_in[0]
    o_z = 2 * qk_w + v_w
    o_a = o_z + v_w
    o_s = o_a + 2 * H
    o_g = o_s + 3 * sb_w
    w_dqkv = wi[:, :o_z].astype(BF16)
    w_dz = wi[:, o_z:o_a].astype(BF16)
    w_ab = jnp.pad(wi[:, o_a:o_s], ((0, 0), (0, 128 - 2 * H))).astype(BF16)
    w_sb = wi[:, o_s:o_g].astype(BF16)
    w_gate = wi[:, o_g:].astype(BF16)
    lane_pad = lambda a: jnp.pad(a.astype(F32), (0, 128 - a.shape[0]))[None, :]
    a_log = lane_pad(dn_a_log[0])
    dt_bias = lane_pad(dn_dt_bias[0])
    sb_gain = jnp.concatenate([jnp.tile(sb_q_norm_gain[0] * (dh ** -0.5), sb_heads),
                               jnp.tile(sb_k_norm_gain[0], sb_heads),
                               jnp.ones((sb_w,), F32)])[None, :].astype(F32)
    conv_w = jnp.concatenate([conv_q[0], conv_k[0], conv_v[0]], axis=1).astype(F32)
    meta_blk = jnp.pad(meta_tokens.astype(F32), ((N_PAD, 0), (0, 0)))
    n_ffc = d_ff // FF_CHUNK
    wfi = w_ffn_in[0].astype(BF16)
    wg = wfi[:, :d_ff].reshape(D, n_ffc, FF_CHUNK).transpose(1, 0, 2)
    wu = wfi[:, d_ff:].reshape(D, n_ffc, FF_CHUNK).transpose(1, 0, 2)
    wo = w_ffn_out[0].astype(BF16).reshape(n_ffc, FF_CHUNK, D)

    hn = _prenorm(x, meta_blk, norm_mix_gain[0][None, :].astype(F32))

    tm_all = _row_tile(rows)
    tm_x = _row_tile(S, 1024)
    dqkv = _proj(hn, w_dqkv, [], _epi_copy, rows=rows, tm=tm_all, tn=1024, out_dtype=F32, name="proj_dqkv")
    dz = _proj(hn, w_dz, [], _epi_silu, rows=S, tm=tm_x, tn=1024, out_dtype=F32, name="proj_dz")
    gb = _proj(hn, w_ab, [a_log, dt_bias],
               functools.partial(_epi_decay_beta, n_heads=H, seq=S, tm=tm_all),
               rows=rows, tm=tm_all, tn=128, out_dtype=F32, name="proj_decay_beta")
    sb_qkv = _proj(hn, w_sb, [sb_gain], functools.partial(_epi_sb_qkv, dh=dh),
                   rows=rows, tm=tm_all, tn=sb_w, out_dtype=BF16, name="proj_sb_qkv")
    gates = _proj(hn, w_gate, [], _epi_sigmoid, rows=S, tm=tm_x, tn=1024, out_dtype=F32, name="proj_gates")

    g_rows = jnp.concatenate([gb[:, S + BLK - CHUNK:, :H], gb[:, :S, :H]], axis=1)
    grow = g_rows.reshape(B, S // CHUNK + 1, CHUNK, H).transpose(0, 1, 3, 2)

    o_dn = _gdn(dqkv, conv_w, gb, grow, dz, dn_out_norm_gain[0][None, :].astype(F32),
                seq=S, n_heads=H, dk=dk, dv=dv)
    o_sb = _stick_breaking(sb_qkv, seq=S, n_heads=sb_heads)

    h1 = _merge(x, o_dn, o_sb, gates, w_branch_dn[0].astype(BF16), w_branch_sb[0].astype(BF16),
                w_out[0].astype(BF16), tm=_row_tile(S, 512))
    return _ffn(h1, norm_ffn_gain[0][None, :].astype(F32), wg, wu, wo, tm=_row_tile(S, 512))
```

```python
import functools

import jax
import jax.numpy as jnp
from jax import lax
from jax.experimental import pallas as pl
from jax.experimental.pallas import tpu as pltpu

F32 = jnp.float32
BF16 = jnp.bfloat16

N_META = 16
CHUNK = 64
BLK = 128
N_PAD = BLK - N_META
RMS_EPS = 1e-6
L2_EPS = 1e-6
NEG_BIG = -1e30
FF_CHUNK = 256
SB_GROUP = 8
SB_EXIT_SUM = 105.0
VMEM_LIMIT = 56 * 1024 * 1024

_NT = (((1,), (1,)), ((), ()))
_TN = (((0,), (0,)), ((), ()))


def _params(*sem):
    return pltpu.CompilerParams(dimension_semantics=sem, vmem_limit_bytes=VMEM_LIMIT)


def _dot(a, b, dims=None):
    if dims is None:
        return jnp.dot(a, b, preferred_element_type=F32)
    return lax.dot_general(a, b, dims, preferred_element_type=F32)


def _bdot(a, b, dims=None):
    return _dot(a.astype(BF16), b.astype(BF16), dims)


def _split(x, n):
    parts = []
    for _ in range(n - 1):
        p = x.astype(BF16)
        parts.append(p)
        x = x - p.astype(F32)
    parts.append(x.astype(BF16))
    return parts


def _sigmoid(x):
    return 1.0 / (1.0 + jnp.exp(-x))


def _softplus(x):
    return jnp.maximum(x, 0.0) + jnp.log(1.0 + jnp.exp(-jnp.abs(x)))


def _row_tile(n, cap=1152):
    best = 16
    for t in range(16, min(n, cap) + 1, 16):
        if n % t == 0:
            best = t
    return best


def _rms(v, gain):
    ms = jnp.mean(v * v, axis=-1, keepdims=True)
    return v * lax.rsqrt(ms + RMS_EPS) * gain


def _prenorm_kernel(x_ref, gain_ref, o_ref):
    o_ref[0] = _rms(x_ref[0], gain_ref[...]).astype(o_ref.dtype)


def _prenorm_meta_kernel(hn_ref, meta_ref, gain_ref, o_ref):
    del hn_ref
    o_ref[0] = _rms(meta_ref[...], gain_ref[...]).astype(o_ref.dtype)


def _prenorm(x, meta_blk, gain):
    B, S, D = x.shape
    tm = _row_tile(S, 1024)
    hn = pl.pallas_call(
        _prenorm_kernel,
        grid=(B, S // tm),
        in_specs=[
            pl.BlockSpec((1, tm, D), lambda b, i: (b, i, 0)),
            pl.BlockSpec((1, D), lambda b, i: (0, 0)),
        ],
        out_specs=pl.BlockSpec((1, tm, D), lambda b, i: (b, i, 0)),
        out_shape=jax.ShapeDtypeStruct((B, S + BLK, D), BF16),
        compiler_params=_params("parallel", "parallel"),
        name="prenorm",
    )(x, gain)
    return pl.pallas_call(
        _prenorm_meta_kernel,
        grid=(B,),
        in_specs=[
            pl.BlockSpec(memory_space=pl.ANY),
            pl.BlockSpec((BLK, D), lambda b: (0, 0)),
            pl.BlockSpec((1, D), lambda b: (0, 0)),
        ],
        out_specs=pl.BlockSpec((1, BLK, D), lambda b: (b, S // BLK, 0)),
        out_shape=jax.ShapeDtypeStruct((B, S + BLK, D), BF16),
        input_output_aliases={0: 0},
        compiler_params=_params("parallel"),
        name="prenorm_meta",
    )(hn, meta_blk, gain)


def _proj_kernel(h_ref, w_ref, *rest, epilogue):
    *aux, o_ref = rest
    acc = _dot(h_ref[0], w_ref[...])
    epilogue(acc, aux, o_ref)


def _proj(hn, w, aux, epilogue, *, rows, tm, tn, out_dtype, name):
    B, _, D = hn.shape
    N = w.shape[1]
    tn = min(tn, N)
    assert N % tn == 0 and rows % tm == 0
    assert all(a.shape == (1, N) for a in aux)
    aux_specs = [pl.BlockSpec((1, tn), lambda b, i, j: (0, j)) for a in aux]
    return pl.pallas_call(
        functools.partial(_proj_kernel, epilogue=epilogue),
        grid=(B, rows // tm, N // tn),
        in_specs=[
            pl.BlockSpec((1, tm, D), lambda b, i, j: (b, i, 0)),
            pl.BlockSpec((D, tn), lambda b, i, j: (0, j)),
        ] + aux_specs,
        out_specs=pl.BlockSpec((1, tm, tn), lambda b, i, j: (b, i, j)),
        out_shape=jax.ShapeDtypeStruct((B, rows, N), out_dtype),
        compiler_params=_params("parallel", "parallel", "arbitrary"),
        name=name,
    )(hn, w, *aux)


def _epi_copy(acc, aux, o_ref):
    o_ref[0] = acc.astype(o_ref.dtype)


def _epi_silu(acc, aux, o_ref):
    o_ref[0] = (acc * _sigmoid(acc)).astype(o_ref.dtype)


def _epi_sigmoid(acc, aux, o_ref):
    o_ref[0] = _sigmoid(acc).astype(o_ref.dtype)


def _epi_decay_beta(acc, aux, o_ref, *, n_heads, seq, tm):
    alog_ref, dtb_ref = aux
    i = pl.program_id(1)
    col = lax.broadcasted_iota(jnp.int32, acc.shape, 1)
    row = i * tm + lax.broadcasted_iota(jnp.int32, acc.shape, 0)
    g = -jnp.exp(alog_ref[...]) * _softplus(acc + dtb_ref[...])
    out = jnp.where(col < n_heads, g, _sigmoid(acc))
    is_pad = (row >= seq) & (row < seq + N_PAD)
    o_ref[0] = jnp.where(is_pad, 0.0, out)


def _epi_sb_qkv(acc, aux, o_ref, *, dh):
    (gain_ref,) = aux
    j = pl.program_id(2)

    @pl.when(j < 2)
    def _():
        for c in range(0, acc.shape[1], dh):
            y = acc[:, c:c + dh]
            ms = jnp.mean(y * y, axis=-1, keepdims=True)
            o_ref[0, :, c:c + dh] = (y * lax.rsqrt(ms + RMS_EPS) * gain_ref[:, c:c + dh]).astype(o_ref.dtype)

    @pl.when(j == 2)
    def _():
        o_ref[0] = acc.astype(o_ref.dtype)


def _gdn_qkv_kernel(h_ref, w_ref, cw_ref, mt_ref, o_ref, tail_ref, *, tm, rc, dk, qk_tiles):
    i = pl.program_id(1)
    j = pl.program_id(2)

    @pl.when((i == 0) & (j == 0))
    def _():
        tail_ref[...] = jnp.zeros(tail_ref.shape, F32)

    acc = _dot(h_ref[0], w_ref[...])
    halo = jnp.where(i == 0, mt_ref[...], tail_ref[j])
    tail_ref[j] = acc[tm - 8:tm, :]

    mul = jnp.where(j < qk_tiles, dk ** -0.5, jnp.where(j < 2 * qk_tiles, 1.0, 0.0))
    add = jnp.where(j < 2 * qk_tiles, 0.0, 1.0)
    for c in range(0, acc.shape[1], dk):
        for r0 in range(0, tm, rc):
            if r0 == 0:
                xe = jnp.concatenate([halo[:, c:c + dk], acc[:rc, c:c + dk]], axis=0)
            else:
                xe = acc[r0 - 8:r0 + rc, c:c + dk]
            h = cw_ref[3:4, c:c + dk] * xe[8:]
            for tap in range(3):
                h = h + cw_ref[tap:tap + 1, c:c + dk] * pltpu.roll(xe, 3 - tap, axis=0)[8:]
            a = h + h * jnp.tanh(h)
            inv = lax.rsqrt(jnp.sum(a * a, axis=-1, keepdims=True) + L2_EPS)
            o_ref[0, r0:r0 + rc, c:c + dk] = (a * (inv * mul + add)).astype(o_ref.dtype)


def _gdn_qkv(hn, w, conv_w, meta_tail, *, qk_w, dk, tm):
    B, rows, D = hn.shape
    N = w.shape[1]
    tn = min(1024, qk_w)
    assert qk_w % tn == 0 and N % tn == 0 and tn % dk == 0 and rows % tm == 0
    rc = _row_tile(tm, 48)
    return pl.pallas_call(
        functools.partial(_gdn_qkv_kernel, tm=tm, rc=rc, dk=dk, qk_tiles=qk_w // tn),
        grid=(B, rows // tm, N // tn),
        in_specs=[
            pl.BlockSpec((1, tm, D), lambda b, i, j: (b, i, 0)),
            pl.BlockSpec((D, tn), lambda b, i, j: (0, j)),
            pl.BlockSpec((4, tn), lambda b, i, j: (0, j)),
            pl.BlockSpec((8, tn), lambda b, i, j: (0, j)),
        ],
        out_specs=pl.BlockSpec((1, tm, tn), lambda b, i, j: (b, i, j)),
        out_shape=jax.ShapeDtypeStruct((B, rows, N), BF16),
        scratch_shapes=[pltpu.VMEM((N // tn, 8, tn), F32)],
        compiler_params=_params("parallel", "arbitrary", "arbitrary"),
        name="proj_gdn_qkv",
    )(hn, w, conv_w, meta_tail)


def _inv_unit_lower(Ls, ixj):
    n = Ls[0].shape[0]
    eye = (ixj == 0).astype(F32)
    Xs = [eye - jnp.where(ixj == 1, L, 0.0) for L in Ls]
    h = 2
    while h < n:
        join = (ixj >= h) & (ixj < 2 * h)
        Xb = [X.astype(BF16) for X in Xs]
        XC = [_dot(xb, jnp.where(join, L, 0.0).astype(BF16)) for xb, L in zip(Xb, Ls)]
        Xs = [X - _dot(xc.astype(BF16), xb) for X, xb, xc in zip(Xs, Xb, XC)]
        h *= 2
    return Xs


def _gdn_kernel(x_ref, gb_ref, grow_ref, dz_ref, gain_ref, o_ref, s_ref, *, n_heads, dk, dv, cps):
    C = CHUNK
    qk_w = n_heads * dk
    heads = range(n_heads)
    subs = range(cps)
    items = [(s, h) for s in subs for h in heads]
    rows = lambda s: slice(s * C, (s + 1) * C)

    @pl.when(pl.program_id(1) == 0)
    def _():
        s_ref[...] = jnp.zeros(s_ref.shape, F32)

    ri = lax.broadcasted_iota(jnp.int32, (C, C), 0)
    ci = lax.broadcasted_iota(jnp.int32, (C, C), 1)
    ixj = ri ^ ci
    incl = ri >= ci
    strict = ri > ci

    tri = incl.astype(BF16)
    tri_t = (ri <= ci).astype(BF16)
    gb = [gb_ref[0, rows(s), :] for s in subs]
    gcum_c = [sum(_dot(tri, p) for p in _split(gb[s], 3)) for s in subs]
    gcum_r = [sum(_dot(p, tri_t) for p in _split(grow_ref[0, s], 3)) for s in subs]

    qb = {(s, h): x_ref[0, rows(s), h * dk:(h + 1) * dk] for s, h in items}
    kb16 = {(s, h): x_ref[0, rows(s), qk_w + h * dk:qk_w + (h + 1) * dk] for s, h in items}
    q = {i: qb[i].astype(F32) for i in items}
    k = {i: kb16[i].astype(F32) for i in items}
    v = {(s, h): x_ref[0, rows(s), 2 * qk_w + h * dv:2 * qk_w + (h + 1) * dv].astype(F32) for s, h in items}
    beta = {(s, h): gb[s][:, n_heads + h:n_heads + h + 1] for s, h in items}
    gc = {(s, h): gcum_c[s][:, h:h + 1] for s, h in items}
    g_last = {(s, h): gcum_c[s][C - 1:C, h:h + 1] for s, h in items}
    decay = {(s, h): jnp.exp(jnp.where(incl, gc[s, h] - gcum_r[s][h:h + 1, :], NEG_BIG))
             for s, h in items}
    kb = {i: k[i] * beta[i] for i in items}
    kq = {i: _dot(jnp.concatenate([kb[i].astype(BF16), qb[i]], axis=0), kb16[i], _NT) for i in items}
    lower = [jnp.where(strict, kq[i][:C] * decay[i], 0.0) for i in items]
    attn = {i: (kq[i][C:] * decay[i]).astype(BF16) for i in items}

    X = dict(zip(items, _inv_unit_lower(lower, ixj)))
    e_g = {i: jnp.exp(gc[i]) for i in items}
    sol = {i: _bdot(X[i], jnp.concatenate([v[i] * beta[i], kb[i] * e_g[i]], axis=1)) for i in items}
    wq = {i: jnp.concatenate([sol[i][:, dv:], q[i] * e_g[i]], axis=0).astype(BF16) for i in items}
    k_dec = {i: (k[i] * jnp.exp(g_last[i] - gc[i])).astype(BF16) for i in items}
    e_last = {i: jnp.exp(g_last[i]) for i in items}

    S = [s_ref[h] for h in heads]
    o = {}
    for s in subs:
        ws_qs = [_dot(wq[s, h], S[h].astype(BF16)) for h in heads]
        v_new = [(sol[s, h][:, :dv] - ws_qs[h][:C]).astype(BF16) for h in heads]
        for h in heads:
            o[s, h] = ws_qs[h][C:] + _dot(attn[s, h], v_new[h])
        S = [S[h] * e_last[s, h] + _dot(k_dec[s, h], v_new[h], _TN) for h in heads]
    for h in heads:
        s_ref[h] = S[h]

    for s, h in items:
        ms = jnp.mean(o[s, h] * o[s, h], axis=-1, keepdims=True)
        on = o[s, h] * lax.rsqrt(ms + RMS_EPS) * gain_ref[...]
        dz = dz_ref[0, rows(s), h * dv:(h + 1) * dv].astype(F32)
        o_ref[0, rows(s), h * dv:(h + 1) * dv] = (on * dz).astype(o_ref.dtype)


def _gdn(qkv, gb, grow, dz, gain, *, seq, n_heads, dk, dv):
    B = qkv.shape[0]
    W = qkv.shape[2]
    cps = BLK // CHUNK
    meta_blk = seq // BLK

    def tl(b, t):
        return (b, jnp.where(t == 0, meta_blk, t - 1), 0)

    def xr(b, t):
        return (b, jnp.maximum(t - 1, 0), 0)

    return pl.pallas_call(
        functools.partial(_gdn_kernel, n_heads=n_heads, dk=dk, dv=dv, cps=cps),
        grid=(B, seq // BLK + 1),
        in_specs=[
            pl.BlockSpec((1, BLK, W), tl),
            pl.BlockSpec((1, BLK, gb.shape[2]), tl),
            pl.BlockSpec((1, cps, n_heads, CHUNK), lambda b, t: (b, t, 0, 0)),
            pl.BlockSpec((1, BLK, n_heads * dv), xr),
            pl.BlockSpec((1, dv), lambda b, t: (0, 0)),
        ],
        out_specs=pl.BlockSpec((1, BLK, n_heads * dv), xr),
        out_shape=jax.ShapeDtypeStruct((B, seq, n_heads * dv), BF16),
        scratch_shapes=[pltpu.VMEM((n_heads, dk, dv), F32)],
        compiler_params=_params("parallel", "arbitrary"),
        name="gdn",
    )(qkv, gb, grow, dz, gain)


def _sb_kernel(q_ref, k_ref, v_ref, o_ref, acc_ref, later_ref, *, n_xblk, group):
    i0 = pl.program_id(2) * group
    chains = range(group)
    ri = lax.broadcasted_iota(jnp.int32, (BLK, BLK), 0)
    ci = lax.broadcasted_iota(jnp.int32, (BLK, BLK), 1)
    diag_bias = jnp.where(ci < ri, 0.0, NEG_BIG)
    lane = lax.broadcasted_iota(jnp.int32, (1, BLK), 1)
    meta_bias = jnp.where(lane >= N_PAD, 0.0, NEG_BIG)
    r2 = lax.broadcasted_iota(jnp.int32, (2 * BLK, 2 * BLK), 0) & (BLK - 1)
    c2 = lax.broadcasted_iota(jnp.int32, (2 * BLK, 2 * BLK), 1)
    tail2 = ((r2 > c2) | (c2 >= BLK)).astype(BF16)

    def trip(jbs, biases, first):
        starts = [pl.multiple_of(jnp.where(jb <= 0, n_xblk * BLK, (jb - 1) * BLK), BLK) for jb in jbs]
        z = [_dot(q_ref[0, c * BLK:(c + 1) * BLK, :], k_ref[0, pl.ds(starts[c], BLK), :], _NT) + biases[c]
             for c in chains]
        sp = [_softplus(z[c]) for c in chains]
        sums = [_dot(jnp.concatenate(_split(sp[c], 2), axis=1), tail2) for c in chains]
        laters = []
        for c in chains:
            if first:
                later = sums[c][:, BLK:]
                acc_ref[c] = _dot(jnp.exp(z[c] - sp[c] - sums[c][:, :BLK]).astype(BF16),
                                  v_ref[0, pl.ds(starts[c], BLK), :])
            else:
                prev = later_ref[c]
                later = prev + sums[c][:, BLK:]
                acc_ref[c] += _dot(jnp.exp(z[c] - sp[c] - (prev + sums[c][:, :BLK])).astype(BF16),
                                   v_ref[0, pl.ds(starts[c], BLK), :])
            later_ref[c] = later
            laters.append(later)
        return laters

    def pending(laters, t_next):
        m = jnp.full((8, BLK), -NEG_BIG, F32)
        for c, later in enumerate(laters):
            col = jnp.min(later.reshape(BLK // 8, 8, BLK), axis=0)
            m = jnp.minimum(m, jnp.where(i0 + c + 1 - t_next >= 0, col, -NEG_BIG))
        return jnp.min(m)

    laters = trip([i0 + c + 1 for c in chains], [diag_bias] * group, True)

    def cond(carry):
        t, low = carry
        return low <= SB_EXIT_SUM

    def body(carry):
        t, _ = carry
        jbs = [i0 + c + 1 - t for c in chains]
        biases = [jnp.where(jb >= 1, 0.0, jnp.where(jb == 0, meta_bias, NEG_BIG)) for jb in jbs]
        return t + 1, pending(trip(jbs, biases, False), t + 1)

    lax.while_loop(cond, body, (jnp.int32(1), pending(laters, 1)))
    for c in range(group):
        o_ref[0, c * BLK:(c + 1) * BLK, :] = acc_ref[c].astype(o_ref.dtype)


def _stick_breaking(qkv, *, seq, n_heads):
    B, rows, _ = qkv.shape
    n_xblk = seq // BLK
    group = max(g for g in range(1, SB_GROUP + 1) if n_xblk % g == 0)
    return pl.pallas_call(
        functools.partial(_sb_kernel, n_xblk=n_xblk, group=group),
        grid=(B, n_heads, n_xblk // group),
        in_specs=[
            pl.BlockSpec((1, group * BLK, BLK), lambda b, h, g: (b, g, h)),
            pl.BlockSpec((1, rows, BLK), lambda b, h, g: (b, 0, n_heads + h)),
            pl.BlockSpec((1, rows, BLK), lambda b, h, g: (b, 0, 2 * n_heads + h)),
        ],
        out_specs=pl.BlockSpec((1, group * BLK, BLK), lambda b, h, g: (b, g, h)),
        out_shape=jax.ShapeDtypeStruct((B, seq, n_heads * BLK), BF16),
        scratch_shapes=[
            pltpu.VMEM((group, BLK, BLK), F32),
            pltpu.VMEM((group, BLK, BLK), F32),
        ],
        compiler_params=_params("parallel", "parallel", "arbitrary"),
        name="stick_breaking",
    )(qkv, qkv, qkv)


def _merge_kernel(x_ref, odn_ref, osb_ref, gate_ref, wdn_ref, wsb_ref, wo_ref, o_ref):
    D = x_ref.shape[2]
    a = _dot(odn_ref[0], wdn_ref[...])
    b = _dot(osb_ref[0], wsb_ref[...])
    merged = gate_ref[0, :, :D].astype(F32) * a + gate_ref[0, :, D:].astype(F32) * b
    o_ref[0] = x_ref[0] + _dot(merged.astype(BF16), wo_ref[...])


def _merge(x, o_dn, o_sb, gates, w_dn, w_sb, w_o, *, tm):
    B, S, D = x.shape
    full = lambda a: pl.BlockSpec(a.shape, lambda b, i: (0,) * a.ndim)
    rows = lambda a: pl.BlockSpec((1, tm, a.shape[2]), lambda b, i: (b, i, 0))
    return pl.pallas_call(
        _merge_kernel,
        grid=(B, S // tm),
        in_specs=[rows(x), rows(o_dn), rows(o_sb), rows(gates), full(w_dn), full(w_sb), full(w_o)],
        out_specs=rows(x),
        out_shape=jax.ShapeDtypeStruct((B, S, D), F32),
        compiler_params=_params("parallel", "parallel"),
        name="merge",
    )(x, o_dn, o_sb, gates, w_dn, w_sb, w_o)


def _ffn_kernel(h_ref, gain_ref, wg_ref, wu_ref, wo_ref, o_ref):
    h = h_ref[0]
    hn = _rms(h, gain_ref[...]).astype(BF16)
    o_ref[0] = h
    for c in range(wg_ref.shape[0]):
        g = _dot(hn, wg_ref[c])
        u = _dot(hn, wu_ref[c])
        act = (g * _sigmoid(g) * u).astype(BF16)
        o_ref[0] += _dot(act, wo_ref[c])


def _ffn(h, gain, wg, wu, wo, *, tm):
    B, S, D = h.shape
    full = lambda a: pl.BlockSpec(a.shape, lambda b, i: (0,) * a.ndim)
    rows = pl.BlockSpec((1, tm, D), lambda b, i: (b, i, 0))
    return pl.pallas_call(
        _ffn_kernel,
        grid=(B, S // tm),
        in_specs=[rows, full(gain), full(wg), full(wu), full(wo)],
        out_specs=rows,
        out_shape=jax.ShapeDtypeStruct((B, S, D), F32),
        compiler_params=_params("parallel", "parallel"),
        name="ffn",
    )(h, gain, wg, wu, wo)


def kernel(x, meta_tokens, norm_mix_gain, w_in, conv_q, conv_k, conv_v, dn_a_log, dn_dt_bias,
           dn_out_norm_gain, sb_q_norm_gain, sb_k_norm_gain, w_branch_dn, w_branch_sb, w_out,
           norm_ffn_gain, w_ffn_in, w_ffn_out):
    B, S, D = x.shape
    assert norm_mix_gain.shape[0] == 1, "one layer: the meta rows of the stream are not carried to a next layer"
    assert meta_tokens.shape[0] == N_META and S % BLK == 0
    H = dn_a_log.shape[-1]
    qk_w, v_w = conv_q.shape[-1], conv_v.shape[-1]
    dk, dv = qk_w // H, v_w // H
    dh = sb_q_norm_gain.shape[-1]
    sb_w = w_branch_sb.shape[-2]
    sb_heads = sb_w // dh
    d_ff = w_ffn_out.shape[-2]
    assert dk % 128 == 0 and dv % 128 == 0 and dh == BLK and 2 * H <= 128 and d_ff % FF_CHUNK == 0
    rows = S + BLK

    wi = w_in[0]
    o_z = 2 * qk_w + v_w
    o_a = o_z + v_w
    o_s = o_a + 2 * H
    o_g = o_s + 3 * sb_w
    w_dqkv = wi[:, :o_z].astype(BF16)
    w_dz = wi[:, o_z:o_a].astype(BF16)
    w_ab = jnp.pad(wi[:, o_a:o_s], ((0, 0), (0, 128 - 2 * H))).astype(BF16)
    w_sb = wi[:, o_s:o_g].astype(BF16)
    w_gate = wi[:, o_g:].astype(BF16)
    lane_pad = lambda a: jnp.pad(a.astype(F32), (0, 128 - a.shape[0]))[None, :]
    a_log = lane_pad(dn_a_log[0])
    dt_bias = lane_pad(dn_dt_bias[0])
    sb_gain = jnp.concatenate([jnp.tile(sb_q_norm_gain[0] * (dh ** -0.5), sb_heads),
                               jnp.tile(sb_k_norm_gain[0], sb_heads),
                               jnp.ones((sb_w,), F32)])[None, :].astype(F32)
    half_conv_w = 0.5 * jnp.concatenate([conv_q[0], conv_k[0], conv_v[0]], axis=1).astype(F32)
    meta_blk = jnp.pad(meta_tokens.astype(F32), ((N_PAD, 0), (0, 0)))
    n_ffc = d_ff // FF_CHUNK
    wfi = w_ffn_in[0].astype(BF16)
    wg = wfi[:, :d_ff].reshape(D, n_ffc, FF_CHUNK).transpose(1, 0, 2)
    wu = wfi[:, d_ff:].reshape(D, n_ffc, FF_CHUNK).transpose(1, 0, 2)
    wo = w_ffn_out[0].astype(BF16).reshape(n_ffc, FF_CHUNK, D)

    hn = _prenorm(x, meta_blk, norm_mix_gain[0][None, :].astype(F32))

    tm_all = _row_tile(rows)
    tm_x = _row_tile(S, 1024)
    meta_proj = _proj(hn[:1, S + N_PAD:], w_dqkv, [], _epi_copy, rows=N_META, tm=N_META, tn=1024,
                      out_dtype=F32, name="proj_meta_rows")
    qkv = _gdn_qkv(hn, w_dqkv, half_conv_w, meta_proj[0, N_META - 8:], qk_w=qk_w, dk=dk, tm=tm_all)
    dz = _proj(hn, w_dz, [], _epi_silu, rows=S, tm=tm_x, tn=1024, out_dtype=BF16, name="proj_dz")
    gb = _proj(hn, w_ab, [a_log, dt_bias],
               functools.partial(_epi_decay_beta, n_heads=H, seq=S, tm=tm_all),
               rows=rows, tm=tm_all, tn=128, out_dtype=F32, name="proj_decay_beta")
    sb_qkv = _proj(hn, w_sb, [sb_gain], functools.partial(_epi_sb_qkv, dh=dh),
                   rows=rows, tm=tm_all, tn=sb_w, out_dtype=BF16, name="proj_sb_qkv")
    gates = _proj(hn, w_gate, [], _epi_sigmoid, rows=S, tm=tm_x, tn=1024, out_dtype=BF16, name="proj_gates")

    g_rows = jnp.concatenate([gb[:, S:, :H], gb[:, :S, :H]], axis=1)
    grow = g_rows.reshape(B, rows // CHUNK, CHUNK, H).transpose(0, 1, 3, 2)

    o_dn = _gdn(qkv, gb, grow, dz, dn_out_norm_gain[0][None, :].astype(F32), seq=S, n_heads=H, dk=dk, dv=dv)
    o_sb = _stick_breaking(sb_qkv, seq=S, n_heads=sb_heads)

    h1 = _merge(x, o_dn, o_sb, gates, w_branch_dn[0].astype(BF16), w_branch_sb[0].astype(BF16),
                w_out[0].astype(BF16), tm=_row_tile(S, 512))
    return _ffn(h1, norm_ffn_gain[0][None, :].astype(F32), wg, wu, wo, tm=_row_tile(S, 512))
```

```python
import functools

import jax
import jax.numpy as jnp
from jax import lax
from jax.experimental import pallas as pl
from jax.experimental.pallas import tpu as pltpu

F32 = jnp.float32
BF16 = jnp.bfloat16

N_META = 16
CHUNK = 64
BLK = 128
GDN_STEP = 256
META_ROWS = GDN_STEP
N_PAD = META_ROWS - N_META
RMS_EPS = 1e-6
L2_EPS = 1e-6
NEG_BIG = -1e30
FF_CHUNK = 256
SB_GROUP = 16
SB_EXIT_SUM = 105.0
VMEM_LIMIT = 56 * 1024 * 1024

_NT = (((1,), (1,)), ((), ()))
_TN = (((0,), (0,)), ((), ()))


def _params(*sem):
    return pltpu.CompilerParams(dimension_semantics=sem, vmem_limit_bytes=VMEM_LIMIT)


def _dot(a, b, dims=None):
    if dims is None:
        return jnp.dot(a, b, preferred_element_type=F32)
    return lax.dot_general(a, b, dims, preferred_element_type=F32)


def _bdot(a, b, dims=None):
    return _dot(a.astype(BF16), b.astype(BF16), dims)


def _split(x, n):
    parts = []
    for _ in range(n - 1):
        p = x.astype(BF16)
        parts.append(p)
        x = x - p.astype(F32)
    parts.append(x.astype(BF16))
    return parts


def _sigmoid(x):
    return 1.0 / (1.0 + jnp.exp(-x))


def _softplus(x):
    return jnp.maximum(x, 0.0) + jnp.log(1.0 + jnp.exp(-jnp.abs(x)))


def _row_tile(n, cap=1152):
    best = 16
    for t in range(16, min(n, cap) + 1, 16):
        if n % t == 0:
            best = t
    return best


def _rms(v, gain):
    ms = jnp.mean(v * v, axis=-1, keepdims=True)
    return v * lax.rsqrt(ms + RMS_EPS) * gain


def _prenorm_kernel(x_ref, gain_ref, o_ref):
    o_ref[0] = _rms(x_ref[0], gain_ref[...]).astype(o_ref.dtype)


def _prenorm_meta_kernel(hn_ref, meta_ref, gain_ref, o_ref):
    del hn_ref
    o_ref[0] = _rms(meta_ref[...], gain_ref[...]).astype(o_ref.dtype)


def _prenorm(x, meta_blk, gain):
    B, S, D = x.shape
    tm = _row_tile(S, 1024)
    hn = pl.pallas_call(
        _prenorm_kernel,
        grid=(B, S // tm),
        in_specs=[
            pl.BlockSpec((1, tm, D), lambda b, i: (b, i, 0)),
            pl.BlockSpec((1, D), lambda b, i: (0, 0)),
        ],
        out_specs=pl.BlockSpec((1, tm, D), lambda b, i: (b, i, 0)),
        out_shape=jax.ShapeDtypeStruct((B, S + META_ROWS, D), BF16),
        compiler_params=_params("parallel", "parallel"),
        name="prenorm",
    )(x, gain)
    return pl.pallas_call(
        _prenorm_meta_kernel,
        grid=(B,),
        in_specs=[
            pl.BlockSpec(memory_space=pl.ANY),
            pl.BlockSpec((META_ROWS, D), lambda b: (0, 0)),
            pl.BlockSpec((1, D), lambda b: (0, 0)),
        ],
        out_specs=pl.BlockSpec((1, META_ROWS, D), lambda b: (b, S // META_ROWS, 0)),
        out_shape=jax.ShapeDtypeStruct((B, S + META_ROWS, D), BF16),
        input_output_aliases={0: 0},
        compiler_params=_params("parallel"),
        name="prenorm_meta",
    )(hn, meta_blk, gain)


def _proj_kernel(h_ref, w_ref, *rest, epilogue):
    *aux, o_ref = rest
    acc = _dot(h_ref[0], w_ref[...])
    epilogue(acc, aux, o_ref)


def _proj(hn, w, aux, epilogue, *, rows, tm, tn, out_dtype, name):
    B, _, D = hn.shape
    N = w.shape[1]
    tn = min(tn, N)
    assert N % tn == 0 and rows % tm == 0
    assert all(a.shape == (1, N) for a in aux)
    aux_specs = [pl.BlockSpec((1, tn), lambda b, i, j: (0, j)) for a in aux]
    return pl.pallas_call(
        functools.partial(_proj_kernel, epilogue=epilogue),
        grid=(B, rows // tm, N // tn),
        in_specs=[
            pl.BlockSpec((1, tm, D), lambda b, i, j: (b, i, 0)),
            pl.BlockSpec((D, tn), lambda b, i, j: (0, j)),
        ] + aux_specs,
        out_specs=pl.BlockSpec((1, tm, tn), lambda b, i, j: (b, i, j)),
        out_shape=jax.ShapeDtypeStruct((B, rows, N), out_dtype),
        compiler_params=_params("parallel", "parallel", "arbitrary"),
        name=name,
    )(hn, w, *aux)


def _epi_copy(acc, aux, o_ref):
    o_ref[0] = acc.astype(o_ref.dtype)


def _epi_silu_of_double(h, aux, o_ref):
    o_ref[0] = (h + h * jnp.tanh(h)).astype(o_ref.dtype)


def _epi_sigmoid_of_double(h, aux, o_ref):
    o_ref[0] = (0.5 * jnp.tanh(h) + 0.5).astype(o_ref.dtype)


def _epi_decay_beta(acc, aux, o_ref, *, n_heads, seq, tm):
    alog_ref, dtb_ref = aux
    i = pl.program_id(1)
    col = lax.broadcasted_iota(jnp.int32, acc.shape, 1)
    row = i * tm + lax.broadcasted_iota(jnp.int32, acc.shape, 0)
    g = -jnp.exp(alog_ref[...]) * _softplus(acc + dtb_ref[...])
    out = jnp.where(col < n_heads, g, _sigmoid(acc))
    is_pad = (row >= seq) & (row < seq + N_PAD)
    o_ref[0] = jnp.where(is_pad, 0.0, out)


def _epi_sb_qkv(acc, aux, o_ref, *, dh):
    (gain_ref,) = aux
    j = pl.program_id(2)
    mul = jnp.where(j < 2, 1.0, 0.0)
    add = jnp.where(j < 2, 0.0, 1.0)
    tm = acc.shape[0]
    rc = _row_tile(tm, 96)
    for c in range(0, acc.shape[1], dh):
        for r0 in range(0, tm, rc):
            y = acc[r0:r0 + rc, c:c + dh]
            inv = lax.rsqrt(jnp.mean(y * y, axis=-1, keepdims=True) + RMS_EPS)
            o_ref[0, r0:r0 + rc, c:c + dh] = (y * (inv * mul + add) * gain_ref[:, c:c + dh]).astype(o_ref.dtype)


def _gdn_qkv_kernel(h_ref, w_ref, cw_ref, mt_ref, o_ref, tail_ref, *, tm, rc, dk, qk_tiles):
    i = pl.program_id(1)
    j = pl.program_id(2)

    @pl.when((i == 0) & (j == 0))
    def _():
        tail_ref[...] = jnp.zeros(tail_ref.shape, F32)

    acc = _dot(h_ref[0], w_ref[...])
    halo = jnp.where(i == 0, mt_ref[...], tail_ref[j])
    tail_ref[j] = acc[tm - 8:tm, :]

    mul = jnp.where(j < qk_tiles, dk ** -0.5, jnp.where(j < 2 * qk_tiles, 1.0, 0.0))
    add = jnp.where(j < 2 * qk_tiles, 0.0, 1.0)
    for c in range(0, acc.shape[1], dk):
        for r0 in range(0, tm, rc):
            if r0 == 0:
                xe = jnp.concatenate([halo[:, c:c + dk], acc[:rc, c:c + dk]], axis=0)
            else:
                xe = acc[r0 - 8:r0 + rc, c:c + dk]
            h = cw_ref[3:4, c:c + dk] * xe[8:]
            for tap in range(3):
                h = h + cw_ref[tap:tap + 1, c:c + dk] * pltpu.roll(xe, 3 - tap, axis=0)[8:]
            a = h + h * jnp.tanh(h)
            inv = lax.rsqrt(jnp.sum(a * a, axis=-1, keepdims=True) + L2_EPS)
            o_ref[0, r0:r0 + rc, c:c + dk] = (a * (inv * mul + add)).astype(o_ref.dtype)


def _gdn_qkv(hn, w, conv_w, meta_tail, *, qk_w, dk, tm):
    B, rows, D = hn.shape
    N = w.shape[1]
    tn = min(1024, qk_w)
    assert qk_w % tn == 0 and N % tn == 0 and tn % dk == 0 and rows % tm == 0
    rc = _row_tile(tm, 48)
    return pl.pallas_call(
        functools.partial(_gdn_qkv_kernel, tm=tm, rc=rc, dk=dk, qk_tiles=qk_w // tn),
        grid=(B, rows // tm, N // tn),
        in_specs=[
            pl.BlockSpec((1, tm, D), lambda b, i, j: (b, i, 0)),
            pl.BlockSpec((D, tn), lambda b, i, j: (0, j)),
            pl.BlockSpec((4, tn), lambda b, i, j: (0, j)),
            pl.BlockSpec((8, tn), lambda b, i, j: (0, j)),
        ],
        out_specs=pl.BlockSpec((1, tm, tn), lambda b, i, j: (b, i, j)),
        out_shape=jax.ShapeDtypeStruct((B, rows, N), BF16),
        scratch_shapes=[pltpu.VMEM((N // tn, 8, tn), F32)],
        compiler_params=_params("parallel", "arbitrary", "arbitrary"),
        name="proj_gdn_qkv",
    )(hn, w, conv_w, meta_tail)


def _inv_unit_lower(Ls, ixj):
    n = Ls[0].shape[0]
    eye = (ixj == 0).astype(F32)
    Xs = [eye - jnp.where(ixj == 1, L, 0.0) for L in Ls]
    h = 2
    while h < n:
        join = (ixj >= h) & (ixj < 2 * h)
        Xb = [X.astype(BF16) for X in Xs]
        XC = [_dot(xb, jnp.where(join, L, 0.0).astype(BF16)) for xb, L in zip(Xb, Ls)]
        Xs = [X - _dot(xc.astype(BF16), xb) for X, xb, xc in zip(Xs, Xb, XC)]
        h *= 2
    return Xs


def _gdn_kernel(x_ref, gb_ref, grow_ref, dz_ref, gain_ref, o_ref, s_ref, *, n_heads, dk, dv, cps):
    C = CHUNK
    qk_w = n_heads * dk
    heads = range(n_heads)
    subs = range(cps)
    items = [(s, h) for s in subs for h in heads]
    rows = lambda s: slice(s * C, (s + 1) * C)

    @pl.when(pl.program_id(1) == 0)
    def _():
        s_ref[...] = jnp.zeros(s_ref.shape, F32)

    ri = lax.broadcasted_iota(jnp.int32, (C, C), 0)
    ci = lax.broadcasted_iota(jnp.int32, (C, C), 1)
    ixj = ri ^ ci
    incl = ri >= ci
    strict = ri > ci

    tri = incl.astype(BF16)
    tri_t = (ri <= ci).astype(BF16)
    gb = [gb_ref[0, rows(s), :] for s in subs]
    gcum_c = [sum(_dot(tri, p) for p in _split(gb[s], 3)) for s in subs]
    gcum_r = [sum(_dot(p, tri_t) for p in _split(grow_ref[0, s], 3)) for s in subs]

    qb = {(s, h): x_ref[0, rows(s), h * dk:(h + 1) * dk] for s, h in items}
    kb16 = {(s, h): x_ref[0, rows(s), qk_w + h * dk:qk_w + (h + 1) * dk] for s, h in items}
    q = {i: qb[i].astype(F32) for i in items}
    k = {i: kb16[i].astype(F32) for i in items}
    v = {(s, h): x_ref[0, rows(s), 2 * qk_w + h * dv:2 * qk_w + (h + 1) * dv].astype(F32) for s, h in items}
    beta = {(s, h): gb[s][:, n_heads + h:n_heads + h + 1] for s, h in items}
    gc = {(s, h): gcum_c[s][:, h:h + 1] for s, h in items}
    g_last = {(s, h): gcum_c[s][C - 1:C, h:h + 1] for s, h in items}
    decay = {(s, h): jnp.exp(jnp.where(incl, gc[s, h] - gcum_r[s][h:h + 1, :], NEG_BIG))
             for s, h in items}
    kb = {i: k[i] * beta[i] for i in items}
    kq = {i: _dot(jnp.concatenate([kb[i].astype(BF16), qb[i]], axis=0), kb16[i], _NT) for i in items}
    lower = [jnp.where(strict, kq[i][:C] * decay[i], 0.0) for i in items]
    attn = {i: (kq[i][C:] * decay[i]).astype(BF16) for i in items}

    X = dict(zip(items, _inv_unit_lower(lower, ixj)))
    e_g = {i: jnp.exp(gc[i]) for i in items}
    sol = {i: _bdot(X[i], jnp.concatenate([v[i] * beta[i], kb[i] * e_g[i]], axis=1)) for i in items}
    wq = {i: jnp.concatenate([sol[i][:, dv:], q[i] * e_g[i]], axis=0).astype(BF16) for i in items}
    k_dec = {i: (k[i] * jnp.exp(g_last[i] - gc[i])).astype(BF16) for i in items}
    e_last = {i: jnp.exp(g_last[i]) for i in items}

    S = [s_ref[h] for h in heads]
    o = {}
    for s in subs:
        ws_qs = [_dot(wq[s, h], S[h].astype(BF16)) for h in heads]
        v_new = [(sol[s, h][:, :dv] - ws_qs[h][:C]).astype(BF16) for h in heads]
        for h in heads:
            o[s, h] = ws_qs[h][C:] + _dot(attn[s, h], v_new[h])
        S = [S[h] * e_last[s, h] + _dot(k_dec[s, h], v_new[h], _TN) for h in heads]
    for h in heads:
        s_ref[h] = S[h]

    for s, h in items:
        ms = jnp.mean(o[s, h] * o[s, h], axis=-1, keepdims=True)
        on = o[s, h] * lax.rsqrt(ms + RMS_EPS) * gain_ref[...]
        dz = dz_ref[0, rows(s), h * dv:(h + 1) * dv].astype(F32)
        o_ref[0, rows(s), h * dv:(h + 1) * dv] = (on * dz).astype(o_ref.dtype)


def _gdn(qkv, gb, grow, dz, gain, *, seq, n_heads, dk, dv):
    B = qkv.shape[0]
    W = qkv.shape[2]
    cps = GDN_STEP // CHUNK
    meta_blk = seq // GDN_STEP

    def tl(b, t):
        return (b, jnp.where(t == 0, meta_blk, t - 1), 0)

    def xr(b, t):
        return (b, jnp.maximum(t - 1, 0), 0)

    return pl.pallas_call(
        functools.partial(_gdn_kernel, n_heads=n_heads, dk=dk, dv=dv, cps=cps),
        grid=(B, seq // GDN_STEP + 1),
        in_specs=[
            pl.BlockSpec((1, GDN_STEP, W), tl),
            pl.BlockSpec((1, GDN_STEP, gb.shape[2]), tl),
            pl.BlockSpec((1, cps, n_heads, CHUNK), lambda b, t: (b, t, 0, 0)),
            pl.BlockSpec((1, GDN_STEP, n_heads * dv), xr),
            pl.BlockSpec((1, dv), lambda b, t: (0, 0)),
        ],
        out_specs=pl.BlockSpec((1, GDN_STEP, n_heads * dv), xr),
        out_shape=jax.ShapeDtypeStruct((B, seq, n_heads * dv), BF16),
        scratch_shapes=[pltpu.VMEM((n_heads, dk, dv), F32)],
        compiler_params=_params("parallel", "arbitrary"),
        name="gdn",
    )(qkv, gb, grow, dz, gain)


def _sb_kernel(q_ref, k_ref, v_ref, o_ref, acc_ref, later_ref, *, meta_start, group):
    i0 = pl.program_id(2) * group
    chains = range(group)
    ri = lax.broadcasted_iota(jnp.int32, (BLK, BLK), 0)
    ci = lax.broadcasted_iota(jnp.int32, (BLK, BLK), 1)
    diag_bias = jnp.where(ci < ri, 0.0, NEG_BIG)
    lane = lax.broadcasted_iota(jnp.int32, (1, BLK), 1)
    meta_bias = jnp.where(lane >= BLK - N_META, 0.0, NEG_BIG)
    r2 = lax.broadcasted_iota(jnp.int32, (2 * BLK, 2 * BLK), 0) & (BLK - 1)
    c2 = lax.broadcasted_iota(jnp.int32, (2 * BLK, 2 * BLK), 1)
    tail2 = ((r2 > c2) | (c2 >= BLK)).astype(BF16)

    def trip(jbs, biases, first):
        starts = [pl.multiple_of(jnp.where(jb <= 0, meta_start, (jb - 1) * BLK), BLK) for jb in jbs]
        z = [_dot(q_ref[0, c * BLK:(c + 1) * BLK, :], k_ref[0, pl.ds(starts[c], BLK), :], _NT) + biases[c]
             for c in chains]
        sp = [_softplus(z[c]) for c in chains]
        sums = [_dot(jnp.concatenate(_split(sp[c], 2), axis=1), tail2) for c in chains]
        laters = []
        for c in chains:
            if first:
                later = sums[c][:, BLK:]
                acc_ref[c] = _dot(jnp.exp(z[c] - sp[c] - sums[c][:, :BLK]).astype(BF16),
                                  v_ref[0, pl.ds(starts[c], BLK), :])
            else:
                prev = later_ref[c]
                later = prev + sums[c][:, BLK:]
                acc_ref[c] += _dot(jnp.exp(z[c] - sp[c] - (prev + sums[c][:, :BLK])).astype(BF16),
                                   v_ref[0, pl.ds(starts[c], BLK), :])
            later_ref[c] = later
            laters.append(later)
        return laters

    def pending(laters, t_next):
        m = jnp.full((8, BLK), -NEG_BIG, F32)
        for c, later in enumerate(laters):
            col = jnp.min(later.reshape(BLK // 8, 8, BLK), axis=0)
            m = jnp.minimum(m, jnp.where(i0 + c + 1 - t_next >= 0, col, -NEG_BIG))
        return jnp.min(m)

    laters = trip([i0 + c + 1 for c in chains], [diag_bias] * group, True)

    def cond(carry):
        t, low = carry
        return low <= SB_EXIT_SUM

    def body(carry):
        t, _ = carry
        jbs = [i0 + c + 1 - t for c in chains]
        biases = [jnp.where(jb >= 1, 0.0, jnp.where(jb == 0, meta_bias, NEG_BIG)) for jb in jbs]
        return t + 1, pending(trip(jbs, biases, False), t + 1)

    lax.while_loop(cond, body, (jnp.int32(1), pending(laters, 1)))
    for c in range(group):
        o_ref[0, c * BLK:(c + 1) * BLK, :] = acc_ref[c].astype(o_ref.dtype)


def _stick_breaking(qkv, *, seq, n_heads):
    B, rows, _ = qkv.shape
    n_xblk = seq // BLK
    group = max(g for g in range(1, SB_GROUP + 1) if n_xblk % g == 0)
    return pl.pallas_call(
        functools.partial(_sb_kernel, meta_start=rows - BLK, group=group),
        grid=(B, n_heads, n_xblk // group),
        in_specs=[
            pl.BlockSpec((1, group * BLK, BLK), lambda b, h, g: (b, g, h)),
            pl.BlockSpec((1, rows, BLK), lambda b, h, g: (b, 0, n_heads + h)),
            pl.BlockSpec((1, rows, BLK), lambda b, h, g: (b, 0, 2 * n_heads + h)),
        ],
        out_specs=pl.BlockSpec((1, group * BLK, BLK), lambda b, h, g: (b, g, h)),
        out_shape=jax.ShapeDtypeStruct((B, seq, n_heads * BLK), BF16),
        scratch_shapes=[
            pltpu.VMEM((group, BLK, BLK), F32),
            pltpu.VMEM((group, BLK, BLK), F32),
        ],
        compiler_params=_params("parallel", "parallel", "arbitrary"),
        name="stick_breaking",
    )(qkv, qkv, qkv)


def _merge_kernel(x_ref, odn_ref, osb_ref, gate_ref, wdn_ref, wsb_ref, wo_ref, o_ref):
    D = x_ref.shape[2]
    a = _dot(odn_ref[0], wdn_ref[...])
    b = _dot(osb_ref[0], wsb_ref[...])
    merged = gate_ref[0, :, :D].astype(F32) * a + gate_ref[0, :, D:].astype(F32) * b
    o_ref[0] = x_ref[0] + _dot(merged.astype(BF16), wo_ref[...])


def _merge(x, o_dn, o_sb, gates, w_dn, w_sb, w_o, *, tm):
    B, S, D = x.shape
    full = lambda a: pl.BlockSpec(a.shape, lambda b, i: (0,) * a.ndim)
    rows = lambda a: pl.BlockSpec((1, tm, a.shape[2]), lambda b, i: (b, i, 0))
    return pl.pallas_call(
        _merge_kernel,
        grid=(B, S // tm),
        in_specs=[rows(x), rows(o_dn), rows(o_sb), rows(gates), full(w_dn), full(w_sb), full(w_o)],
        out_specs=rows(x),
        out_shape=jax.ShapeDtypeStruct((B, S, D), F32),
        compiler_params=_params("parallel", "parallel"),
        name="merge",
    )(x, o_dn, o_sb, gates, w_dn, w_sb, w_o)


def _ffn_kernel(h_ref, gain_ref, wi_ref, wo_ref, o_ref):
    d_ff = wo_ref.shape[0]
    h = h_ref[0]
    hn = _rms(h, gain_ref[...]).astype(BF16)
    o_ref[0] = h
    for c in range(0, d_ff, FF_CHUNK):
        g = _dot(hn, wi_ref[:, c:c + FF_CHUNK])
        u = _dot(hn, wi_ref[:, d_ff + c:d_ff + c + FF_CHUNK])
        act = (g * _sigmoid(g) * u).astype(BF16)
        o_ref[0] += _dot(act, wo_ref[c:c + FF_CHUNK, :])


def _ffn(h, gain, w_in, w_out, *, tm):
    B, S, D = h.shape
    full = lambda a: pl.BlockSpec(a.shape, lambda b, i: (0,) * a.ndim)
    rows = pl.BlockSpec((1, tm, D), lambda b, i: (b, i, 0))
    return pl.pallas_call(
        _ffn_kernel,
        grid=(B, S // tm),
        in_specs=[rows, full(gain), full(w_in), full(w_out)],
        out_specs=rows,
        out_shape=jax.ShapeDtypeStruct((B, S, D), F32),
        compiler_params=_params("parallel", "parallel"),
        name="ffn",
    )(h, gain, w_in, w_out)


def kernel(x, meta_tokens, norm_mix_gain, w_in, conv_q, conv_k, conv_v, dn_a_log, dn_dt_bias,
           dn_out_norm_gain, sb_q_norm_gain, sb_k_norm_gain, w_branch_dn, w_branch_sb, w_out,
           norm_ffn_gain, w_ffn_in, w_ffn_out):
    B, S, D = x.shape
    assert norm_mix_gain.shape[0] == 1, "one layer: the meta rows of the stream are not carried to a next layer"
    assert meta_tokens.shape[0] == N_META and S % GDN_STEP == 0
    H = dn_a_log.shape[-1]
    qk_w, v_w = conv_q.shape[-1], conv_v.shape[-1]
    dk, dv = qk_w // H, v_w // H
    dh = sb_q_norm_gain.shape[-1]
    sb_w = w_branch_sb.shape[-2]
    sb_heads = sb_w // dh
    d_ff = w_ffn_out.shape[-2]
    assert dk % 128 == 0 and dv % 128 == 0 and dh == BLK and 2 * H <= 128 and d_ff % FF_CHUNK == 0
    rows = S + META_ROWS

    wi = w_in[0]
    o_z = 2 * qk_w + v_w
    o_a = o_z + v_w
    o_s = o_a + 2 * H
    o_g = o_s + 3 * sb_w
    w_dqkv = wi[:, :o_z].astype(BF16)
    half_w_dz = (0.5 * wi[:, o_z:o_a]).astype(BF16)
    w_ab = jnp.pad(wi[:, o_a:o_s], ((0, 0), (0, 128 - 2 * H))).astype(BF16)
    w_sb = wi[:, o_s:o_g].astype(BF16)
    half_w_gate = (0.5 * wi[:, o_g:]).astype(BF16)
    lane_pad = lambda a: jnp.pad(a.astype(F32), (0, 128 - a.shape[0]))[None, :]
    a_log = lane_pad(dn_a_log[0])
    dt_bias = lane_pad(dn_dt_bias[0])
    sb_gain = jnp.concatenate([jnp.tile(sb_q_norm_gain[0] * (dh ** -0.5), sb_heads),
                               jnp.tile(sb_k_norm_gain[0], sb_heads),
                               jnp.ones((sb_w,), F32)])[None, :].astype(F32)
    half_conv_w = 0.5 * jnp.concatenate([conv_q[0], conv_k[0], conv_v[0]], axis=1).astype(F32)
    meta_blk = jnp.pad(meta_tokens.astype(F32), ((N_PAD, 0), (0, 0)))

    hn = _prenorm(x, meta_blk, norm_mix_gain[0][None, :].astype(F32))

    tm_all = _row_tile(rows)
    tm_x = _row_tile(S, 1024)
    meta_proj = _proj(hn[:1, S + N_PAD:], w_dqkv, [], _epi_copy, rows=N_META, tm=N_META, tn=1024,
                      out_dtype=F32, name="proj_meta_rows")
    qkv = _gdn_qkv(hn, w_dqkv, half_conv_w, meta_proj[0, N_META - 8:], qk_w=qk_w, dk=dk, tm=tm_all)
    dz = _proj(hn, half_w_dz, [], _epi_silu_of_double, rows=S, tm=tm_x, tn=1024, out_dtype=BF16, name="proj_dz")
    gb = _proj(hn, w_ab, [a_log, dt_bias],
               functools.partial(_epi_decay_beta, n_heads=H, seq=S, tm=tm_all),
               rows=rows, tm=tm_all, tn=128, out_dtype=F32, name="proj_decay_beta")
    sb_qkv = _proj(hn, w_sb, [sb_gain], functools.partial(_epi_sb_qkv, dh=dh),
                   rows=rows, tm=tm_all, tn=sb_w, out_dtype=BF16, name="proj_sb_qkv")
    gates = _proj(hn, half_w_gate, [], _epi_sigmoid_of_double, rows=S, tm=tm_x, tn=1024, out_dtype=BF16,
                   name="proj_gates")

    g_rows = jnp.concatenate([gb[:, S:, :H], gb[:, :S, :H]], axis=1)
    grow = g_rows.reshape(B, rows // CHUNK, CHUNK, H).transpose(0, 1, 3, 2)

    o_dn = _gdn(qkv, gb, grow, dz, dn_out_norm_gain[0][None, :].astype(F32), seq=S, n_heads=H, dk=dk, dv=dv)
    o_sb = _stick_breaking(sb_qkv, seq=S, n_heads=sb_heads)

    h1 = _merge(x, o_dn, o_sb, gates, w_branch_dn[0].astype(BF16), w_branch_sb[0].astype(BF16),
                w_out[0].astype(BF16), tm=_row_tile(S, 512))
    return _ffn(h1, norm_ffn_gain[0][None, :].astype(F32), w_ffn_in[0].astype(BF16), w_ffn_out[0].astype(BF16),
                tm=_row_tile(S, 512))
```

```python
import functools

import jax
import jax.numpy as jnp
from jax import lax
from jax.experimental import pallas as pl
from jax.experimental.pallas import tpu as pltpu

F32 = jnp.float32
BF16 = jnp.bfloat16

N_META = 16
CHUNK = 64
BLK = 128
GDN_STEP = 256
META_ROWS = GDN_STEP
N_PAD = META_ROWS - N_META
RMS_EPS = 1e-6
L2_EPS = 1e-6
NEG_BIG = -1e30
FF_CHUNK = 256
SB_GROUP = 16
SB_EXIT_SUM = 105.0
SB_HEAD = 32
VMEM_LIMIT = 56 * 1024 * 1024

_NT = (((1,), (1,)), ((), ()))
_TN = (((0,), (0,)), ((), ()))


def _params(*sem):
    return pltpu.CompilerParams(dimension_semantics=sem, vmem_limit_bytes=VMEM_LIMIT)


def _dot(a, b, dims=None):
    if dims is None:
        return jnp.dot(a, b, preferred_element_type=F32)
    return lax.dot_general(a, b, dims, preferred_element_type=F32)


def _bdot(a, b, dims=None):
    return _dot(a.astype(BF16), b.astype(BF16), dims)


def _split(x, n):
    parts = []
    for _ in range(n - 1):
        p = x.astype(BF16)
        parts.append(p)
        x = x - p.astype(F32)
    parts.append(x.astype(BF16))
    return parts


def _sigmoid(x):
    return 1.0 / (1.0 + jnp.exp(-x))


def _softplus(x):
    return jnp.maximum(x, 0.0) + jnp.log(1.0 + jnp.exp(-jnp.abs(x)))


def _row_tile(n, cap=1152):
    best = 16
    for t in range(16, min(n, cap) + 1, 16):
        if n % t == 0:
            best = t
    return best


def _rms(v, gain):
    ms = jnp.mean(v * v, axis=-1, keepdims=True)
    return v * lax.rsqrt(ms + RMS_EPS) * gain


def _prenorm_kernel(x_ref, meta_ref, gain_ref, o_ref, *, n_xtiles):
    i = pl.program_id(1)

    @pl.when(i < n_xtiles)
    def _():
        o_ref[0] = _rms(x_ref[0], gain_ref[...]).astype(o_ref.dtype)

    @pl.when(i == n_xtiles)
    def _():
        o_ref[0] = _rms(meta_ref[...], gain_ref[...]).astype(o_ref.dtype)


def _prenorm(x, meta_blk, gain):
    B, S, D = x.shape
    n_xtiles = S // META_ROWS
    return pl.pallas_call(
        functools.partial(_prenorm_kernel, n_xtiles=n_xtiles),
        grid=(B, n_xtiles + 1),
        in_specs=[
            pl.BlockSpec((1, META_ROWS, D), lambda b, i: (b, jnp.minimum(i, n_xtiles - 1), 0)),
            pl.BlockSpec((META_ROWS, D), lambda b, i: (0, 0)),
            pl.BlockSpec((1, D), lambda b, i: (0, 0)),
        ],
        out_specs=pl.BlockSpec((1, META_ROWS, D), lambda b, i: (b, i, 0)),
        out_shape=jax.ShapeDtypeStruct((B, S + META_ROWS, D), BF16),
        compiler_params=_params("parallel", "arbitrary"),
        name="prenorm",
    )(x, meta_blk, gain)


def _proj_kernel(h_ref, w_ref, *rest, epilogue):
    *aux, o_ref = rest
    acc = _dot(h_ref[0], w_ref[...])
    epilogue(acc, aux, o_ref)


def _proj(hn, w, aux, epilogue, *, rows, tm, tn, out_dtype, name):
    B, _, D = hn.shape
    N = w.shape[1]
    tn = min(tn, N)
    assert N % tn == 0 and rows % tm == 0
    assert all(a.shape == (1, N) for a in aux)
    aux_specs = [pl.BlockSpec((1, tn), lambda b, i, j: (0, j)) for a in aux]
    return pl.pallas_call(
        functools.partial(_proj_kernel, epilogue=epilogue),
        grid=(B, rows // tm, N // tn),
        in_specs=[
            pl.BlockSpec((1, tm, D), lambda b, i, j: (b, i, 0)),
            pl.BlockSpec((D, tn), lambda b, i, j: (0, j)),
        ] + aux_specs,
        out_specs=pl.BlockSpec((1, tm, tn), lambda b, i, j: (b, i, j)),
        out_shape=jax.ShapeDtypeStruct((B, rows, N), out_dtype),
        compiler_params=_params("parallel", "parallel", "arbitrary"),
        name=name,
    )(hn, w, *aux)


def _epi_copy(acc, aux, o_ref):
    o_ref[0] = acc.astype(o_ref.dtype)


def _epi_silu_of_double(h, aux, o_ref):
    o_ref[0] = (h + h * jnp.tanh(h)).astype(o_ref.dtype)


def _epi_sigmoid_of_double(h, aux, o_ref):
    o_ref[0] = (0.5 * jnp.tanh(h) + 0.5).astype(o_ref.dtype)


def _epi_decay_beta(acc, aux, o_ref, *, n_heads, seq, tm):
    alog_ref, dtb_ref = aux
    i = pl.program_id(1)
    col = lax.broadcasted_iota(jnp.int32, acc.shape, 1)
    row = i * tm + lax.broadcasted_iota(jnp.int32, acc.shape, 0)
    g = -jnp.exp(alog_ref[...]) * _softplus(acc + dtb_ref[...])
    out = jnp.where(col < n_heads, g, _sigmoid(acc))
    is_pad = (row >= seq) & (row < seq + N_PAD)
    o_ref[0] = jnp.where(is_pad, 0.0, out)


def _epi_sb_qkv(acc, aux, o_ref, *, dh):
    (gain_ref,) = aux
    j = pl.program_id(2)
    mul = jnp.where(j < 2, 1.0, 0.0)
    add = jnp.where(j < 2, 0.0, 1.0)
    tm = acc.shape[0]
    rc = _row_tile(tm, 96)
    for c in range(0, acc.shape[1], dh):
        for r0 in range(0, tm, rc):
            y = acc[r0:r0 + rc, c:c + dh]
            inv = lax.rsqrt(jnp.mean(y * y, axis=-1, keepdims=True) + RMS_EPS)
            o_ref[0, r0:r0 + rc, c:c + dh] = (y * (inv * mul + add) * gain_ref[:, c:c + dh]).astype(o_ref.dtype)


def _gdn_qkv_kernel(h_ref, w_ref, cw_ref, mt_ref, o_ref, tail_ref, *, tm, rc, dk, qk_tiles):
    i = pl.program_id(1)
    j = pl.program_id(2)

    @pl.when((i == 0) & (j == 0))
    def _():
        tail_ref[...] = jnp.zeros(tail_ref.shape, F32)

    acc = _dot(h_ref[0], w_ref[...])
    halo = jnp.where(i == 0, mt_ref[...], tail_ref[j])
    tail_ref[j] = acc[tm - 8:tm, :]

    mul = jnp.where(j < qk_tiles, dk ** -0.5, jnp.where(j < 2 * qk_tiles, 1.0, 0.0))
    add = jnp.where(j < 2 * qk_tiles, 0.0, 1.0)
    for c in range(0, acc.shape[1], dk):
        for r0 in range(0, tm, rc):
            if r0 == 0:
                xe = jnp.concatenate([halo[:, c:c + dk], acc[:rc, c:c + dk]], axis=0)
            else:
                xe = acc[r0 - 8:r0 + rc, c:c + dk]
            h = cw_ref[3:4, c:c + dk] * xe[8:]
            for tap in range(3):
                h = h + cw_ref[tap:tap + 1, c:c + dk] * pltpu.roll(xe, 3 - tap, axis=0)[8:]
            a = h + h * jnp.tanh(h)
            inv = lax.rsqrt(jnp.sum(a * a, axis=-1, keepdims=True) + L2_EPS)
            o_ref[0, r0:r0 + rc, c:c + dk] = (a * (inv * mul + add)).astype(o_ref.dtype)


def _gdn_qkv(hn, w, conv_w, meta_tail, *, qk_w, dk, tm):
    B, rows, D = hn.shape
    N = w.shape[1]
    tn = min(1024, qk_w)
    assert qk_w % tn == 0 and N % tn == 0 and tn % dk == 0 and rows % tm == 0
    rc = _row_tile(tm, 64)
    return pl.pallas_call(
        functools.partial(_gdn_qkv_kernel, tm=tm, rc=rc, dk=dk, qk_tiles=qk_w // tn),
        grid=(B, rows // tm, N // tn),
        in_specs=[
            pl.BlockSpec((1, tm, D), lambda b, i, j: (b, i, 0)),
            pl.BlockSpec((D, tn), lambda b, i, j: (0, j)),
            pl.BlockSpec((4, tn), lambda b, i, j: (0, j)),
            pl.BlockSpec((8, tn), lambda b, i, j: (0, j)),
        ],
        out_specs=pl.BlockSpec((1, tm, tn), lambda b, i, j: (b, i, j)),
        out_shape=jax.ShapeDtypeStruct((B, rows, N), BF16),
        scratch_shapes=[pltpu.VMEM((N // tn, 8, tn), F32)],
        compiler_params=_params("parallel", "arbitrary", "arbitrary"),
        name="proj_gdn_qkv",
    )(hn, w, conv_w, meta_tail)


def _inv_unit_lower(Ls, ixj):
    n = Ls[0].shape[0]
    eye = (ixj == 0).astype(F32)
    Xs = [eye - jnp.where(ixj == 1, L, 0.0) for L in Ls]
    h = 2
    while h < n:
        join = (ixj >= h) & (ixj < 2 * h)
        Xb = [X.astype(BF16) for X in Xs]
        XC = [_dot(xb, jnp.where(join, L, 0.0).astype(BF16)) for xb, L in zip(Xb, Ls)]
        Xs = [X - _dot(xc.astype(BF16), xb) for X, xb, xc in zip(Xs, Xb, XC)]
        h *= 2
    return Xs


def _gdn_kernel(x_ref, gb_ref, grow_ref, dz_ref, gain_ref, o_ref, s_ref, *, n_heads, dk, dv, cps):
    C = CHUNK
    qk_w = n_heads * dk
    heads = range(n_heads)
    subs = range(cps)
    items = [(s, h) for s in subs for h in heads]
    rows = lambda s: slice(s * C, (s + 1) * C)

    @pl.when(pl.program_id(1) == 0)
    def _():
        s_ref[...] = jnp.zeros(s_ref.shape, F32)

    ri = lax.broadcasted_iota(jnp.int32, (C, C), 0)
    ci = lax.broadcasted_iota(jnp.int32, (C, C), 1)
    ixj = ri ^ ci
    incl = ri >= ci
    strict = ri > ci

    tri = incl.astype(BF16)
    tri_t = (ri <= ci).astype(BF16)
    gb = [gb_ref[0, rows(s), :] for s in subs]
    gcum_c = [sum(_dot(tri, p) for p in _split(gb[s], 3)) for s in subs]
    gcum_r = [sum(_dot(p, tri_t) for p in _split(grow_ref[0, s], 3)) for s in subs]

    qb = {(s, h): x_ref[0, rows(s), h * dk:(h + 1) * dk] for s, h in items}
    kb16 = {(s, h): x_ref[0, rows(s), qk_w + h * dk:qk_w + (h + 1) * dk] for s, h in items}
    q = {i: qb[i].astype(F32) for i in items}
    k = {i: kb16[i].astype(F32) for i in items}
    v = {(s, h): x_ref[0, rows(s), 2 * qk_w + h * dv:2 * qk_w + (h + 1) * dv].astype(F32) for s, h in items}
    beta = {(s, h): gb[s][:, n_heads + h:n_heads + h + 1] for s, h in items}
    gc = {(s, h): gcum_c[s][:, h:h + 1] for s, h in items}
    g_last = {(s, h): gcum_c[s][C - 1:C, h:h + 1] for s, h in items}
    decay = {(s, h): jnp.exp(jnp.where(incl, gc[s, h] - gcum_r[s][h:h + 1, :], NEG_BIG))
             for s, h in items}
    kb = {i: k[i] * beta[i] for i in items}
    kq = {i: _dot(jnp.concatenate([kb[i].astype(BF16), qb[i]], axis=0), kb16[i], _NT) for i in items}
    lower = [jnp.where(strict, kq[i][:C] * decay[i], 0.0) for i in items]
    attn = {i: (kq[i][C:] * decay[i]).astype(BF16) for i in items}

    X = dict(zip(items, _inv_unit_lower(lower, ixj)))
    e_g = {i: jnp.exp(gc[i]) for i in items}
    sol = {i: _bdot(X[i], jnp.concatenate([v[i] * beta[i], kb[i] * e_g[i]], axis=1)) for i in items}
    wq = {i: jnp.concatenate([sol[i][:, dv:], q[i] * e_g[i]], axis=0).astype(BF16) for i in items}
    k_dec = {i: (k[i] * jnp.exp(g_last[i] - gc[i])).astype(BF16) for i in items}
    e_last = {i: jnp.exp(g_last[i]) for i in items}

    S = [s_ref[h] for h in heads]
    o = {}
    for s in subs:
        ws_qs = [_dot(wq[s, h], S[h].astype(BF16)) for h in heads]
        v_new = [(sol[s, h][:, :dv] - ws_qs[h][:C]).astype(BF16) for h in heads]
        for h in heads:
            o[s, h] = ws_qs[h][C:] + _dot(attn[s, h], v_new[h])
        S = [S[h] * e_last[s, h] + _dot(k_dec[s, h], v_new[h], _TN) for h in heads]
    for h in heads:
        s_ref[h] = S[h]

    for s, h in items:
        ms = jnp.mean(o[s, h] * o[s, h], axis=-1, keepdims=True)
        on = o[s, h] * lax.rsqrt(ms + RMS_EPS) * gain_ref[...]
        dz = dz_ref[0, rows(s), h * dv:(h + 1) * dv].astype(F32)
        o_ref[0, rows(s), h * dv:(h + 1) * dv] = (on * dz).astype(o_ref.dtype)


def _gdn(qkv, gb, grow, dz, gain, *, seq, n_heads, dk, dv):
    B = qkv.shape[0]
    W = qkv.shape[2]
    cps = GDN_STEP // CHUNK
    meta_blk = seq // GDN_STEP

    def tl(b, t):
        return (b, jnp.where(t == 0, meta_blk, t - 1), 0)

    def xr(b, t):
        return (b, jnp.maximum(t - 1, 0), 0)

    return pl.pallas_call(
        functools.partial(_gdn_kernel, n_heads=n_heads, dk=dk, dv=dv, cps=cps),
        grid=(B, seq // GDN_STEP + 1),
        in_specs=[
            pl.BlockSpec((1, GDN_STEP, W), tl),
            pl.BlockSpec((1, GDN_STEP, gb.shape[2]), tl),
            pl.BlockSpec((1, cps, n_heads, CHUNK), lambda b, t: (b, t, 0, 0)),
            pl.BlockSpec((1, GDN_STEP, n_heads * dv), xr),
            pl.BlockSpec((1, dv), lambda b, t: (0, 0)),
        ],
        out_specs=pl.BlockSpec((1, GDN_STEP, n_heads * dv), xr),
        out_shape=jax.ShapeDtypeStruct((B, seq, n_heads * dv), BF16),
        scratch_shapes=[pltpu.VMEM((n_heads, dk, dv), F32)],
        compiler_params=_params("parallel", "arbitrary"),
        name="gdn",
    )(qkv, gb, grow, dz, gain)


def _sb_kernel(q_ref, k_ref, v_ref, o_ref, acc_ref, later_ref, *, meta_start, group):
    i0 = pl.program_id(2) * group
    chains = range(group)
    ri = lax.broadcasted_iota(jnp.int32, (BLK, BLK), 0)
    ci = lax.broadcasted_iota(jnp.int32, (BLK, BLK), 1)
    diag_bias = jnp.where(ci < ri, 0.0, NEG_BIG)
    lane = lax.broadcasted_iota(jnp.int32, (1, BLK), 1)
    meta_bias = jnp.where(lane >= BLK - N_META, 0.0, NEG_BIG)
    r2 = lax.broadcasted_iota(jnp.int32, (2 * BLK, 2 * BLK), 0) & (BLK - 1)
    c2 = lax.broadcasted_iota(jnp.int32, (2 * BLK, 2 * BLK), 1)
    tail2 = ((r2 > c2) | (c2 >= BLK)).astype(BF16)

    def trip(jbs, biases, first, n_rows):
        starts = [pl.multiple_of(jnp.where(jb <= 0, meta_start, (jb - 1) * BLK), BLK) for jb in jbs]
        z = [_dot(q_ref[0, c * BLK:c * BLK + n_rows, :], k_ref[0, pl.ds(starts[c], BLK), :], _NT) + biases[c]
             for c in chains]
        sp = [_softplus(z[c]) for c in chains]
        sums = [_dot(jnp.concatenate(_split(sp[c], 2), axis=1), tail2) for c in chains]
        laters = []
        for c in chains:
            if first:
                later = sums[c][:, BLK:]
                acc_ref[c, :n_rows, :] = _dot(jnp.exp(z[c] - sp[c] - sums[c][:, :BLK]).astype(BF16),
                                              v_ref[0, pl.ds(starts[c], BLK), :])
            else:
                prev = later_ref[c, :n_rows, :]
                later = prev + sums[c][:, BLK:]
                acc_ref[c, :n_rows, :] += _dot(jnp.exp(z[c] - sp[c] - (prev + sums[c][:, :BLK])).astype(BF16),
                                               v_ref[0, pl.ds(starts[c], BLK), :])
            later_ref[c, :n_rows, :] = later
            laters.append(later)
        return laters

    def pending(laters, t_next):
        m = jnp.full((8, BLK), -NEG_BIG, F32)
        for c, later in enumerate(laters):
            col = jnp.min(later.reshape(later.shape[0] // 8, 8, BLK), axis=0)
            m = jnp.minimum(m, jnp.where(i0 + c + 1 - t_next >= 0, col, -NEG_BIG))
        return jnp.min(m)

    def walk(t0, low0, n_rows, watch):
        def body(carry):
            t, _ = carry
            jbs = [i0 + c + 1 - t for c in chains]
            biases = [jnp.where(jb >= 1, 0.0, jnp.where(jb == 0, meta_bias, NEG_BIG)) for jb in jbs]
            laters = trip(jbs, biases, False, n_rows)
            return t + 1, pending([later[watch] for later in laters], t + 1)

        return lax.while_loop(lambda carry: carry[1] <= SB_EXIT_SUM, body, (t0, low0))[0]

    head, rest = slice(0, SB_HEAD), slice(SB_HEAD, BLK)
    laters = trip([i0 + c + 1 for c in chains], [diag_bias] * group, True, BLK)
    t = walk(jnp.int32(1), pending([later[rest] for later in laters], 1), BLK, rest)
    walk(t, pending([later_ref[c, head, :] for c in chains], t), SB_HEAD, head)
    for c in range(group):
        o_ref[0, c * BLK:(c + 1) * BLK, :] = acc_ref[c].astype(o_ref.dtype)


def _stick_breaking(qkv, *, seq, n_heads):
    B, rows, _ = qkv.shape
    n_xblk = seq // BLK
    group = max(g for g in range(1, SB_GROUP + 1) if n_xblk % g == 0)
    return pl.pallas_call(
        functools.partial(_sb_kernel, meta_start=rows - BLK, group=group),
        grid=(B, n_heads, n_xblk // group),
        in_specs=[
            pl.BlockSpec((1, group * BLK, BLK), lambda b, h, g: (b, g, h)),
            pl.BlockSpec((1, rows, BLK), lambda b, h, g: (b, 0, n_heads + h)),
            pl.BlockSpec((1, rows, BLK), lambda b, h, g: (b, 0, 2 * n_heads + h)),
        ],
        out_specs=pl.BlockSpec((1, group * BLK, BLK), lambda b, h, g: (b, g, h)),
        out_shape=jax.ShapeDtypeStruct((B, seq, n_heads * BLK), BF16),
        scratch_shapes=[
            pltpu.VMEM((group, BLK, BLK), F32),
            pltpu.VMEM((group, BLK, BLK), F32),
        ],
        compiler_params=_params("parallel", "parallel", "arbitrary"),
        name="stick_breaking",
    )(qkv, qkv, qkv)


def _merge_kernel(x_ref, odn_ref, osb_ref, gate_ref, wdn_ref, wsb_ref, wo_ref, o_ref):
    D = x_ref.shape[2]
    a = _dot(odn_ref[0], wdn_ref[...])
    b = _dot(osb_ref[0], wsb_ref[...])
    merged = gate_ref[0, :, :D].astype(F32) * a + gate_ref[0, :, D:].astype(F32) * b
    o_ref[0] = x_ref[0] + _dot(merged.astype(BF16), wo_ref[...])


def _merge(x, o_dn, o_sb, gates, w_dn, w_sb, w_o, *, tm):
    B, S, D = x.shape
    full = lambda a: pl.BlockSpec(a.shape, lambda b, i: (0,) * a.ndim)
    rows = lambda a: pl.BlockSpec((1, tm, a.shape[2]), lambda b, i: (b, i, 0))
    return pl.pallas_call(
        _merge_kernel,
        grid=(B, S // tm),
        in_specs=[rows(x), rows(o_dn), rows(o_sb), rows(gates), full(w_dn), full(w_sb), full(w_o)],
        out_specs=rows(x),
        out_shape=jax.ShapeDtypeStruct((B, S, D), F32),
        compiler_params=_params("parallel", "parallel"),
        name="merge",
    )(x, o_dn, o_sb, gates, w_dn, w_sb, w_o)


def _ffn_kernel(h_ref, gain_ref, wi_ref, wo_ref, o_ref):
    d_ff = wo_ref.shape[0]
    h = h_ref[0]
    hn = _rms(h, gain_ref[...]).astype(BF16)
    o_ref[0] = h
    for c in range(0, d_ff, FF_CHUNK):
        g = _dot(hn, wi_ref[:, c:c + FF_CHUNK])
        u = _dot(hn, wi_ref[:, d_ff + c:d_ff + c + FF_CHUNK])
        act = (g * _sigmoid(g) * u).astype(BF16)
        o_ref[0] += _dot(act, wo_ref[c:c + FF_CHUNK, :])


def _ffn(h, gain, w_in, w_out, *, tm):
    B, S, D = h.shape
    full = lambda a: pl.BlockSpec(a.shape, lambda b, i: (0,) * a.ndim)
    rows = pl.BlockSpec((1, tm, D), lambda b, i: (b, i, 0))
    return pl.pallas_call(
        _ffn_kernel,
        grid=(B, S // tm),
        in_specs=[rows, full(gain), full(w_in), full(w_out)],
        out_specs=rows,
        out_shape=jax.ShapeDtypeStruct((B, S, D), F32),
        compiler_params=_params("parallel", "parallel"),
        name="ffn",
    )(h, gain, w_in, w_out)


def kernel(x, meta_tokens, norm_mix_gain, w_in, conv_q, conv_k, conv_v, dn_a_log, dn_dt_bias,
           dn_out_norm_gain, sb_q_norm_gain, sb_k_norm_gain, w_branch_dn, w_branch_sb, w_out,
           norm_ffn_gain, w_ffn_in, w_ffn_out):
    B, S, D = x.shape
    assert norm_mix_gain.shape[0] == 1, "one layer: the meta rows of the stream are not carried to a next layer"
    assert meta_tokens.shape[0] == N_META and S % GDN_STEP == 0
    H = dn_a_log.shape[-1]
    qk_w, v_w = conv_q.shape[-1], conv_v.shape[-1]
    dk, dv = qk_w // H, v_w // H
    dh = sb_q_norm_gain.shape[-1]
    sb_w = w_branch_sb.shape[-2]
    sb_heads = sb_w // dh
    d_ff = w_ffn_out.shape[-2]
    assert dk % 128 == 0 and dv % 128 == 0 and dh == BLK and 2 * H <= 128 and d_ff % FF_CHUNK == 0
    rows = S + META_ROWS

    wi = w_in[0]
    o_z = 2 * qk_w + v_w
    o_a = o_z + v_w
    o_s = o_a + 2 * H
    o_g = o_s + 3 * sb_w
    w_dqkv = wi[:, :o_z].astype(BF16)
    half_w_dz = (0.5 * wi[:, o_z:o_a]).astype(BF16)
    w_ab = jnp.pad(wi[:, o_a:o_s], ((0, 0), (0, 128 - 2 * H))).astype(BF16)
    w_sb = wi[:, o_s:o_g].astype(BF16)
    half_w_gate = (0.5 * wi[:, o_g:]).astype(BF16)
    lane_pad = lambda a: jnp.pad(a.astype(F32), (0, 128 - a.shape[0]))[None, :]
    a_log = lane_pad(dn_a_log[0])
    dt_bias = lane_pad(dn_dt_bias[0])
    sb_gain = jnp.concatenate([jnp.tile(sb_q_norm_gain[0] * (dh ** -0.5), sb_heads),
                               jnp.tile(sb_k_norm_gain[0], sb_heads),
                               jnp.ones((sb_w,), F32)])[None, :].astype(F32)
    half_conv_w = 0.5 * jnp.concatenate([conv_q[0], conv_k[0], conv_v[0]], axis=1).astype(F32)
    meta_blk = jnp.pad(meta_tokens.astype(F32), ((N_PAD, 0), (0, 0)))

    hn = _prenorm(x, meta_blk, norm_mix_gain[0][None, :].astype(F32))

    tm_all = _row_tile(rows)
    tm_x = _row_tile(S, 1024)
    meta_proj = _proj(hn[:1, S + N_PAD:], w_dqkv, [], _epi_copy, rows=N_META, tm=N_META, tn=1024,
                      out_dtype=F32, name="proj_meta_rows")
    qkv = _gdn_qkv(hn, w_dqkv, half_conv_w, meta_proj[0, N_META - 8:], qk_w=qk_w, dk=dk, tm=tm_all)
    dz = _proj(hn, half_w_dz, [], _epi_silu_of_double, rows=S, tm=tm_x, tn=1024, out_dtype=BF16, name="proj_dz")
    gb = _proj(hn, w_ab, [a_log, dt_bias],
               functools.partial(_epi_decay_beta, n_heads=H, seq=S, tm=tm_all),
               rows=rows, tm=tm_all, tn=128, out_dtype=F32, name="proj_decay_beta")
    sb_qkv = _proj(hn, w_sb, [sb_gain], functools.partial(_epi_sb_qkv, dh=dh),
                   rows=rows, tm=tm_all, tn=sb_w, out_dtype=BF16, name="proj_sb_qkv")
    gates = _proj(hn, half_w_gate, [], _epi_sigmoid_of_double, rows=S, tm=tm_x, tn=1024, out_dtype=BF16,
                   name="proj_gates")

    g_rows = jnp.concatenate([gb[:, S:, :H], gb[:, :S, :H]], axis=1)
    grow = g_rows.reshape(B, rows // CHUNK, CHUNK, H).transpose(0, 1, 3, 2)

    o_dn = _gdn(qkv, gb, grow, dz, dn_out_norm_gain[0][None, :].astype(F32), seq=S, n_heads=H, dk=dk, dv=dv)
    o_sb = _stick_breaking(sb_qkv, seq=S, n_heads=sb_heads)

    h1 = _merge(x, o_dn, o_sb, gates, w_branch_dn[0].astype(BF16), w_branch_sb[0].astype(BF16),
                w_out[0].astype(BF16), tm=_row_tile(S, 512))
    return _ffn(h1, norm_ffn_gain[0][None, :].astype(F32), w_ffn_in[0].astype(BF16), w_ffn_out[0].astype(BF16),
                tm=_row_tile(S, 512))
```

```python
import functools

import jax
import jax.numpy as jnp
from jax import lax
from jax.experimental import pallas as pl
from jax.experimental.pallas import tpu as pltpu

F32 = jnp.float32
BF16 = jnp.bfloat16

N_META = 16
CHUNK = 64
BLK = 128
GDN_STEP = 256
META_ROWS = GDN_STEP
N_PAD = META_ROWS - N_META
RMS_EPS = 1e-6
L2_EPS = 1e-6
NEG_BIG = -1e30
FF_CHUNK = 256
SB_GROUP = 16
SB_EXIT_SUM = 105.0
SB_HEAD = 32
VMEM_LIMIT = 56 * 1024 * 1024

_NT = (((1,), (1,)), ((), ()))
_TN = (((0,), (0,)), ((), ()))


def _params(*sem):
    return pltpu.CompilerParams(dimension_semantics=sem, vmem_limit_bytes=VMEM_LIMIT)


def _dot(a, b, dims=None):
    if dims is None:
        return jnp.dot(a, b, preferred_element_type=F32)
    return lax.dot_general(a, b, dims, preferred_element_type=F32)


def _bdot(a, b, dims=None):
    return _dot(a.astype(BF16), b.astype(BF16), dims)


def _split(x, n):
    parts = []
    for _ in range(n - 1):
        p = x.astype(BF16)
        parts.append(p)
        x = x - p.astype(F32)
    parts.append(x.astype(BF16))
    return parts


def _sigmoid(x):
    return 1.0 / (1.0 + jnp.exp(-x))


def _softplus(x):
    return jnp.maximum(x, 0.0) + jnp.log(1.0 + jnp.exp(-jnp.abs(x)))


def _row_tile(n, cap=1152):
    best = 16
    for t in range(16, min(n, cap) + 1, 16):
        if n % t == 0:
            best = t
    return best


def _rms(v, gain):
    ms = jnp.mean(v * v, axis=-1, keepdims=True)
    return v * lax.rsqrt(ms + RMS_EPS) * gain


def _prenorm_kernel(x_ref, gain_ref, o_ref):
    o_ref[0] = _rms(x_ref[0], gain_ref[...]).astype(o_ref.dtype)


def _prenorm_meta_kernel(hn_ref, meta_ref, gain_ref, o_ref):
    del hn_ref
    o_ref[0] = _rms(meta_ref[...], gain_ref[...]).astype(o_ref.dtype)


def _prenorm(x, meta_blk, gain):
    B, S, D = x.shape
    tm = _row_tile(S, 1024)
    hn = pl.pallas_call(
        _prenorm_kernel,
        grid=(B, S // tm),
        in_specs=[
            pl.BlockSpec((1, tm, D), lambda b, i: (b, i, 0)),
            pl.BlockSpec((1, D), lambda b, i: (0, 0)),
        ],
        out_specs=pl.BlockSpec((1, tm, D), lambda b, i: (b, i, 0)),
        out_shape=jax.ShapeDtypeStruct((B, S + META_ROWS, D), BF16),
        compiler_params=_params("parallel", "parallel"),
        name="prenorm",
    )(x, gain)
    return pl.pallas_call(
        _prenorm_meta_kernel,
        grid=(B,),
        in_specs=[
            pl.BlockSpec(memory_space=pl.ANY),
            pl.BlockSpec((META_ROWS, D), lambda b: (0, 0)),
            pl.BlockSpec((1, D), lambda b: (0, 0)),
        ],
        out_specs=pl.BlockSpec((1, META_ROWS, D), lambda b: (b, S // META_ROWS, 0)),
        out_shape=jax.ShapeDtypeStruct((B, S + META_ROWS, D), BF16),
        input_output_aliases={0: 0},
        compiler_params=_params("parallel"),
        name="prenorm_meta",
    )(hn, meta_blk, gain)


def _proj_kernel(h_ref, wt_ref, *rest, epilogue):
    *aux, o_ref = rest
    acc = _dot(h_ref[0], wt_ref[...], _NT)
    epilogue(acc, aux, o_ref)


def _proj(hn, wt, aux, epilogue, *, rows, tm, tn, out_dtype, name):
    B, _, D = hn.shape
    N = wt.shape[0]
    tn = min(tn, N)
    assert N % tn == 0 and rows % tm == 0
    assert all(a.shape == (1, N) for a in aux)
    aux_specs = [pl.BlockSpec((1, tn), lambda b, i, j: (0, j)) for a in aux]
    return pl.pallas_call(
        functools.partial(_proj_kernel, epilogue=epilogue),
        grid=(B, rows // tm, N // tn),
        in_specs=[
            pl.BlockSpec((1, tm, D), lambda b, i, j: (b, i, 0)),
            pl.BlockSpec((tn, D), lambda b, i, j: (j, 0)),
        ] + aux_specs,
        out_specs=pl.BlockSpec((1, tm, tn), lambda b, i, j: (b, i, j)),
        out_shape=jax.ShapeDtypeStruct((B, rows, N), out_dtype),
        compiler_params=_params("parallel", "parallel", "arbitrary"),
        name=name,
    )(hn, wt, *aux)


def _epi_copy(acc, aux, o_ref):
    o_ref[0] = acc.astype(o_ref.dtype)


def _epi_silu_of_double(h, aux, o_ref):
    o_ref[0] = (h + h * jnp.tanh(h)).astype(o_ref.dtype)


def _epi_sigmoid_of_double(h, aux, o_ref):
    o_ref[0] = (0.5 * jnp.tanh(h) + 0.5).astype(o_ref.dtype)


def _decay_beta_kernel(h_ref, wt_ref, alog_ref, dtb_ref, o_ref, ot_ref, *, n_heads, seq, tm):
    i = pl.program_id(1)
    acc = _dot(h_ref[0], wt_ref[...], _NT)
    col = lax.broadcasted_iota(jnp.int32, acc.shape, 1)
    row = i * tm + lax.broadcasted_iota(jnp.int32, acc.shape, 0)
    g = -jnp.exp(alog_ref[...]) * _softplus(acc + dtb_ref[...])
    out = jnp.where(col < n_heads, g, _sigmoid(acc))
    is_pad = (row >= seq) & (row < seq + N_PAD)
    out = jnp.where(is_pad, 0.0, out)
    o_ref[0] = out
    ot_ref[0] = out.T


def _decay_beta(hn, wt, a_log, dt_bias, *, n_heads, seq):
    B, rows, D = hn.shape
    tm = GDN_STEP
    vec = pl.BlockSpec((1, 128), lambda b, i: (0, 0))
    return pl.pallas_call(
        functools.partial(_decay_beta_kernel, n_heads=n_heads, seq=seq, tm=tm),
        grid=(B, rows // tm),
        in_specs=[pl.BlockSpec((1, tm, D), lambda b, i: (b, i, 0)), pl.BlockSpec((128, D), lambda b, i: (0, 0)), vec, vec],
        out_specs=[pl.BlockSpec((1, tm, 128), lambda b, i: (b, i, 0)), pl.BlockSpec((1, 128, tm), lambda b, i: (b, 0, i))],
        out_shape=[jax.ShapeDtypeStruct((B, rows, 128), F32), jax.ShapeDtypeStruct((B, 128, rows), F32)],
        compiler_params=_params("parallel", "parallel"),
        name="proj_decay_beta",
    )(hn, wt, a_log, dt_bias)


def _epi_sb_qkv(acc, aux, o_ref, *, dh):
    (gain_ref,) = aux
    j = pl.program_id(2)
    mul = jnp.where(j < 2, 1.0, 0.0)
    add = jnp.where(j < 2, 0.0, 1.0)
    tm = acc.shape[0]
    rc = _row_tile(tm, 96)
    for c in range(0, acc.shape[1], dh):
        for r0 in range(0, tm, rc):
            y = acc[r0:r0 + rc, c:c + dh]
            inv = lax.rsqrt(jnp.mean(y * y, axis=-1, keepdims=True) + RMS_EPS)
            o_ref[0, r0:r0 + rc, c:c + dh] = (y * (inv * mul + add) * gain_ref[:, c:c + dh]).astype(o_ref.dtype)


def _gdn_qkv_kernel(h_ref, wt_ref, cw_ref, mt_ref, o_ref, tail_ref, *, tm, rc, dk, qk_tiles):
    i = pl.program_id(1)
    j = pl.program_id(2)

    @pl.when((i == 0) & (j == 0))
    def _():
        tail_ref[...] = jnp.zeros(tail_ref.shape, F32)

    acc = _dot(h_ref[0], wt_ref[...], _NT)
    halo = jnp.where(i == 0, mt_ref[...], tail_ref[j])
    tail_ref[j] = acc[tm - 8:tm, :]

    mul = jnp.where(j < qk_tiles, dk ** -0.5, jnp.where(j < 2 * qk_tiles, 1.0, 0.0))
    add = jnp.where(j < 2 * qk_tiles, 0.0, 1.0)
    for c in range(0, acc.shape[1], dk):
        for r0 in range(0, tm, rc):
            if r0 == 0:
                xe = jnp.concatenate([halo[:, c:c + dk], acc[:rc, c:c + dk]], axis=0)
            else:
                xe = acc[r0 - 8:r0 + rc, c:c + dk]
            h = cw_ref[3:4, c:c + dk] * xe[8:]
            for tap in range(3):
                h = h + cw_ref[tap:tap + 1, c:c + dk] * pltpu.roll(xe, 3 - tap, axis=0)[8:]
            a = h + h * jnp.tanh(h)
            inv = lax.rsqrt(jnp.sum(a * a, axis=-1, keepdims=True) + L2_EPS)
            o_ref[0, r0:r0 + rc, c:c + dk] = (a * (inv * mul + add)).astype(o_ref.dtype)


def _gdn_qkv(hn, wt, conv_w, meta_tail, *, qk_w, dk, tm):
    B, rows, D = hn.shape
    N = wt.shape[0]
    tn = min(1024, qk_w)
    assert qk_w % tn == 0 and N % tn == 0 and tn % dk == 0 and rows % tm == 0
    rc = _row_tile(tm, 64)
    return pl.pallas_call(
        functools.partial(_gdn_qkv_kernel, tm=tm, rc=rc, dk=dk, qk_tiles=qk_w // tn),
        grid=(B, rows // tm, N // tn),
        in_specs=[
            pl.BlockSpec((1, tm, D), lambda b, i, j: (b, i, 0)),
            pl.BlockSpec((tn, D), lambda b, i, j: (j, 0)),
            pl.BlockSpec((4, tn), lambda b, i, j: (0, j)),
            pl.BlockSpec((8, tn), lambda b, i, j: (0, j)),
        ],
        out_specs=pl.BlockSpec((1, tm, tn), lambda b, i, j: (b, i, j)),
        out_shape=jax.ShapeDtypeStruct((B, rows, N), BF16),
        scratch_shapes=[pltpu.VMEM((N // tn, 8, tn), F32)],
        compiler_params=_params("parallel", "arbitrary", "arbitrary"),
        name="proj_gdn_qkv",
    )(hn, wt, conv_w, meta_tail)


def _inv_unit_lower(Ls, ixj):
    n = Ls[0].shape[0]
    eye = (ixj == 0).astype(F32)
    Xs = [eye - jnp.where(ixj == 1, L, 0.0) for L in Ls]
    h = 2
    while h < n:
        join = (ixj >= h) & (ixj < 2 * h)
        Xb = [X.astype(BF16) for X in Xs]
        XC = [_dot(xb, jnp.where(join, L, 0.0).astype(BF16)) for xb, L in zip(Xb, Ls)]
        Xs = [X - _dot(xc.astype(BF16), xb) for X, xb, xc in zip(Xs, Xb, XC)]
        h *= 2
    return Xs


def _gdn_kernel(x_ref, gb_ref, grow_ref, dz_ref, gain_ref, o_ref, s_ref, *, n_heads, dk, dv, cps):
    C = CHUNK
    qk_w = n_heads * dk
    heads = range(n_heads)
    subs = range(cps)
    items = [(s, h) for s in subs for h in heads]
    rows = lambda s: slice(s * C, (s + 1) * C)

    @pl.when(pl.program_id(1) == 0)
    def _():
        s_ref[...] = jnp.zeros(s_ref.shape, F32)

    ri = lax.broadcasted_iota(jnp.int32, (C, C), 0)
    ci = lax.broadcasted_iota(jnp.int32, (C, C), 1)
    ixj = ri ^ ci
    incl = ri >= ci
    strict = ri > ci

    tri = incl.astype(BF16)
    tri_t = (ri <= ci).astype(BF16)
    gb = [gb_ref[0, rows(s), :] for s in subs]
    gcum_c = [sum(_dot(tri, p) for p in _split(gb[s], 3)) for s in subs]
    gcum_r = [sum(_dot(p, tri_t) for p in _split(grow_ref[0, :, rows(s)], 3)) for s in subs]

    qb = {(s, h): x_ref[0, rows(s), h * dk:(h + 1) * dk] for s, h in items}
    kb16 = {(s, h): x_ref[0, rows(s), qk_w + h * dk:qk_w + (h + 1) * dk] for s, h in items}
    q = {i: qb[i].astype(F32) for i in items}
    k = {i: kb16[i].astype(F32) for i in items}
    v = {(s, h): x_ref[0, rows(s), 2 * qk_w + h * dv:2 * qk_w + (h + 1) * dv].astype(F32) for s, h in items}
    beta = {(s, h): gb[s][:, n_heads + h:n_heads + h + 1] for s, h in items}
    gc = {(s, h): gcum_c[s][:, h:h + 1] for s, h in items}
    g_last = {(s, h): gcum_c[s][C - 1:C, h:h + 1] for s, h in items}
    decay = {(s, h): jnp.exp(jnp.where(incl, gc[s, h] - gcum_r[s][h:h + 1, :], NEG_BIG))
             for s, h in items}
    kb = {i: k[i] * beta[i] for i in items}
    kq = {i: _dot(jnp.concatenate([kb[i].astype(BF16), qb[i]], axis=0), kb16[i], _NT) for i in items}
    lower = [jnp.where(strict, kq[i][:C] * decay[i], 0.0) for i in items]
    attn = {i: (kq[i][C:] * decay[i]).astype(BF16) for i in items}

    X = dict(zip(items, _inv_unit_lower(lower, ixj)))
    e_g = {i: jnp.exp(gc[i]) for i in items}
    sol = {i: _bdot(X[i], jnp.concatenate([v[i] * beta[i], kb[i] * e_g[i]], axis=1)) for i in items}
    wq = {i: jnp.concatenate([sol[i][:, dv:], q[i] * e_g[i]], axis=0).astype(BF16) for i in items}
    k_dec = {i: (k[i] * jnp.exp(g_last[i] - gc[i])).astype(BF16) for i in items}
    e_last = {i: jnp.exp(g_last[i]) for i in items}

    S = [s_ref[h] for h in heads]
    o = {}
    for s in subs:
        ws_qs = [_dot(wq[s, h], S[h].astype(BF16)) for h in heads]
        v_new = [(sol[s, h][:, :dv] - ws_qs[h][:C]).astype(BF16) for h in heads]
        for h in heads:
            o[s, h] = ws_qs[h][C:] + _dot(attn[s, h], v_new[h])
        S = [S[h] * e_last[s, h] + _dot(k_dec[s, h], v_new[h], _TN) for h in heads]
    for h in heads:
        s_ref[h] = S[h]

    for s, h in items:
        ms = jnp.mean(o[s, h] * o[s, h], axis=-1, keepdims=True)
        on = o[s, h] * lax.rsqrt(ms + RMS_EPS) * gain_ref[...]
        dz = dz_ref[0, rows(s), h * dv:(h + 1) * dv].astype(F32)
        o_ref[0, rows(s), h * dv:(h + 1) * dv] = (on * dz).astype(o_ref.dtype)


def _gdn(qkv, gb, grow, dz, gain, *, seq, n_heads, dk, dv):
    B = qkv.shape[0]
    W = qkv.shape[2]
    cps = GDN_STEP // CHUNK
    meta_blk = seq // GDN_STEP

    def tl(b, t):
        return (b, jnp.where(t == 0, meta_blk, t - 1), 0)

    def xr(b, t):
        return (b, jnp.maximum(t - 1, 0), 0)

    return pl.pallas_call(
        functools.partial(_gdn_kernel, n_heads=n_heads, dk=dk, dv=dv, cps=cps),
        grid=(B, seq // GDN_STEP + 1),
        in_specs=[
            pl.BlockSpec((1, GDN_STEP, W), tl),
            pl.BlockSpec((1, GDN_STEP, gb.shape[2]), tl),
            pl.BlockSpec((1, -(-n_heads // 8) * 8, GDN_STEP), lambda b, t: (b, 0, jnp.where(t == 0, meta_blk, t - 1))),
            pl.BlockSpec((1, GDN_STEP, n_heads * dv), xr),
            pl.BlockSpec((1, dv), lambda b, t: (0, 0)),
        ],
        out_specs=pl.BlockSpec((1, GDN_STEP, n_heads * dv), xr),
        out_shape=jax.ShapeDtypeStruct((B, seq, n_heads * dv), BF16),
        scratch_shapes=[pltpu.VMEM((n_heads, dk, dv), F32)],
        compiler_params=_params("parallel", "arbitrary"),
        name="gdn",
    )(qkv, gb, grow, dz, gain)


def _sb_kernel(q_ref, k_ref, v_ref, o_ref, acc_ref, later_ref, *, meta_start, group):
    i0 = pl.program_id(2) * group
    chains = range(group)
    ri = lax.broadcasted_iota(jnp.int32, (BLK, BLK), 0)
    ci = lax.broadcasted_iota(jnp.int32, (BLK, BLK), 1)
    diag_bias = jnp.where(ci < ri, 0.0, NEG_BIG)
    lane = lax.broadcasted_iota(jnp.int32, (1, BLK), 1)
    meta_bias = jnp.where(lane >= BLK - N_META, 0.0, NEG_BIG)
    r2 = lax.broadcasted_iota(jnp.int32, (2 * BLK, 2 * BLK), 0) & (BLK - 1)
    c2 = lax.broadcasted_iota(jnp.int32, (2 * BLK, 2 * BLK), 1)
    tail2 = ((r2 > c2) | (c2 >= BLK)).astype(BF16)

    def trip(jbs, biases, first, n_rows):
        starts = [pl.multiple_of(jnp.where(jb <= 0, meta_start, (jb - 1) * BLK), BLK) for jb in jbs]
        z = [_dot(q_ref[0, c * BLK:c * BLK + n_rows, :], k_ref[0, pl.ds(starts[c], BLK), :], _NT) + biases[c]
             for c in chains]
        sp = [_softplus(z[c]) for c in chains]
        sums = [_dot(jnp.concatenate(_split(sp[c], 2), axis=1), tail2) for c in chains]
        laters = []
        for c in chains:
            if first:
                later = sums[c][:, BLK:]
                acc_ref[c, :n_rows, :] = _dot(jnp.exp(z[c] - sp[c] - sums[c][:, :BLK]).astype(BF16),
                                              v_ref[0, pl.ds(starts[c], BLK), :])
            else:
                prev = later_ref[c, :n_rows, :]
                later = prev + sums[c][:, BLK:]
                acc_ref[c, :n_rows, :] += _dot(jnp.exp(z[c] - sp[c] - (prev + sums[c][:, :BLK])).astype(BF16),
                                               v_ref[0, pl.ds(starts[c], BLK), :])
            later_ref[c, :n_rows, :] = later
            laters.append(later)
        return laters

    def pending(laters, t_next):
        m = jnp.full((8, BLK), -NEG_BIG, F32)
        for c, later in enumerate(laters):
            col = jnp.min(later.reshape(later.shape[0] // 8, 8, BLK), axis=0)
            m = jnp.minimum(m, jnp.where(i0 + c + 1 - t_next >= 0, col, -NEG_BIG))
        return jnp.min(m)

    def walk(t0, low0, n_rows, watch):
        def body(carry):
            t, _ = carry
            jbs = [i0 + c + 1 - t for c in chains]
            biases = [jnp.where(jb >= 1, 0.0, jnp.where(jb == 0, meta_bias, NEG_BIG)) for jb in jbs]
            laters = trip(jbs, biases, False, n_rows)
            return t + 1, pending([later[watch] for later in laters], t + 1)

        return lax.while_loop(lambda carry: carry[1] <= SB_EXIT_SUM, body, (t0, low0))[0]

    head, rest = slice(0, SB_HEAD), slice(SB_HEAD, BLK)
    laters = trip([i0 + c + 1 for c in chains], [diag_bias] * group, True, BLK)
    t = walk(jnp.int32(1), pending([later[rest] for later in laters], 1), BLK, rest)
    walk(t, pending([later_ref[c, head, :] for c in chains], t), SB_HEAD, head)
    for c in range(group):
        o_ref[0, c * BLK:(c + 1) * BLK, :] = acc_ref[c].astype(o_ref.dtype)


def _stick_breaking(qkv, *, seq, n_heads):
    B, rows, _ = qkv.shape
    n_xblk = seq // BLK
    group = max(g for g in range(1, SB_GROUP + 1) if n_xblk % g == 0)
    return pl.pallas_call(
        functools.partial(_sb_kernel, meta_start=rows - BLK, group=group),
        grid=(B, n_heads, n_xblk // group),
        in_specs=[
            pl.BlockSpec((1, group * BLK, BLK), lambda b, h, g: (b, g, h)),
            pl.BlockSpec((1, rows, BLK), lambda b, h, g: (b, 0, n_heads + h)),
            pl.BlockSpec((1, rows, BLK), lambda b, h, g: (b, 0, 2 * n_heads + h)),
        ],
        out_specs=pl.BlockSpec((1, group * BLK, BLK), lambda b, h, g: (b, g, h)),
        out_shape=jax.ShapeDtypeStruct((B, seq, n_heads * BLK), BF16),
        scratch_shapes=[
            pltpu.VMEM((group, BLK, BLK), F32),
            pltpu.VMEM((group, BLK, BLK), F32),
        ],
        compiler_params=_params("parallel", "parallel", "arbitrary"),
        name="stick_breaking",
    )(qkv, qkv, qkv)


def _merge_kernel(x_ref, odn_ref, osb_ref, gate_ref, wdn_ref, wsb_ref, wo_ref, o_ref):
    D = x_ref.shape[2]
    a = _dot(odn_ref[0], wdn_ref[...])
    b = _dot(osb_ref[0], wsb_ref[...])
    merged = gate_ref[0, :, :D].astype(F32) * a + gate_ref[0, :, D:].astype(F32) * b
    o_ref[0] = x_ref[0] + _dot(merged.astype(BF16), wo_ref[...])


def _merge(x, o_dn, o_sb, gates, w_dn, w_sb, w_o, *, tm):
    B, S, D = x.shape
    full = lambda a: pl.BlockSpec(a.shape, lambda b, i: (0,) * a.ndim)
    rows = lambda a: pl.BlockSpec((1, tm, a.shape[2]), lambda b, i: (b, i, 0))
    return pl.pallas_call(
        _merge_kernel,
        grid=(B, S // tm),
        in_specs=[rows(x), rows(o_dn), rows(o_sb), rows(gates), full(w_dn), full(w_sb), full(w_o)],
        out_specs=rows(x),
        out_shape=jax.ShapeDtypeStruct((B, S, D), F32),
        compiler_params=_params("parallel", "parallel"),
        name="merge",
    )(x, o_dn, o_sb, gates, w_dn, w_sb, w_o)


def _ffn_kernel(h_ref, gain_ref, wi_ref, wo_ref, o_ref):
    d_ff = wo_ref.shape[0]
    h = h_ref[0]
    hn = _rms(h, gain_ref[...]).astype(BF16)
    o_ref[0] = h
    for c in range(0, d_ff, FF_CHUNK):
        g = _dot(hn, wi_ref[:, c:c + FF_CHUNK])
        u = _dot(hn, wi_ref[:, d_ff + c:d_ff + c + FF_CHUNK])
        act = (g * _sigmoid(g) * u).astype(BF16)
        o_ref[0] += _dot(act, wo_ref[c:c + FF_CHUNK, :])


def _ffn(h, gain, w_in, w_out, *, tm):
    B, S, D = h.shape
    full = lambda a: pl.BlockSpec(a.shape, lambda b, i: (0,) * a.ndim)
    rows = pl.BlockSpec((1, tm, D), lambda b, i: (b, i, 0))
    return pl.pallas_call(
        _ffn_kernel,
        grid=(B, S // tm),
        in_specs=[rows, full(gain), full(w_in), full(w_out)],
        out_specs=rows,
        out_shape=jax.ShapeDtypeStruct((B, S, D), F32),
        compiler_params=_params("parallel", "parallel"),
        name="ffn",
    )(h, gain, w_in, w_out)


def kernel(x, meta_tokens, norm_mix_gain, w_in, conv_q, conv_k, conv_v, dn_a_log, dn_dt_bias,
           dn_out_norm_gain, sb_q_norm_gain, sb_k_norm_gain, w_branch_dn, w_branch_sb, w_out,
           norm_ffn_gain, w_ffn_in, w_ffn_out):
    B, S, D = x.shape
    assert norm_mix_gain.shape[0] == 1, "one layer: the meta rows of the stream are not carried to a next layer"
    assert meta_tokens.shape[0] == N_META and S % GDN_STEP == 0
    H = dn_a_log.shape[-1]
    qk_w, v_w = conv_q.shape[-1], conv_v.shape[-1]
    dk, dv = qk_w // H, v_w // H
    dh = sb_q_norm_gain.shape[-1]
    sb_w = w_branch_sb.shape[-2]
    sb_heads = sb_w // dh
    d_ff = w_ffn_out.shape[-2]
    assert dk % 128 == 0 and dv % 128 == 0 and dh == BLK and 2 * H <= 128 and d_ff % FF_CHUNK == 0
    rows = S + META_ROWS

    wt = jnp.swapaxes(w_in[0], 0, 1)
    o_z = 2 * qk_w + v_w
    o_a = o_z + v_w
    o_s = o_a + 2 * H
    o_g = o_s + 3 * sb_w
    w_dqkv = wt[:o_z].astype(BF16)
    half_w_dz = (0.5 * wt[o_z:o_a]).astype(BF16)
    w_ab = jnp.pad(wt[o_a:o_s], ((0, 128 - 2 * H), (0, 0))).astype(BF16)
    w_sb = wt[o_s:o_g].astype(BF16)
    half_w_gate = (0.5 * wt[o_g:]).astype(BF16)
    lane_pad = lambda a: jnp.pad(a.astype(F32), (0, 128 - a.shape[0]))[None, :]
    a_log = lane_pad(dn_a_log[0])
    dt_bias = lane_pad(dn_dt_bias[0])
    sb_gain = jnp.concatenate([jnp.tile(sb_q_norm_gain[0] * (dh ** -0.5), sb_heads),
                               jnp.tile(sb_k_norm_gain[0], sb_heads),
                               jnp.ones((sb_w,), F32)])[None, :].astype(F32)
    half_conv_w = 0.5 * jnp.concatenate([conv_q[0], conv_k[0], conv_v[0]], axis=1).astype(F32)
    meta_blk = jnp.pad(meta_tokens.astype(F32), ((N_PAD, 0), (0, 0)))

    hn = _prenorm(x, meta_blk, norm_mix_gain[0][None, :].astype(F32))

    tm_all = _row_tile(rows)
    tm_x = _row_tile(S, 1024)
    meta_proj = _proj(hn[:1, S + N_PAD:], w_dqkv, [], _epi_copy, rows=N_META, tm=N_META, tn=1024,
                      out_dtype=F32, name="proj_meta_rows")
    qkv = _gdn_qkv(hn, w_dqkv, half_conv_w, meta_proj[0, N_META - 8:], qk_w=qk_w, dk=dk, tm=tm_all)
    dz = _proj(hn, half_w_dz, [], _epi_silu_of_double, rows=S, tm=tm_x, tn=1024, out_dtype=BF16, name="proj_dz")
    gb, gb_t = _decay_beta(hn, w_ab, a_log, dt_bias, n_heads=H, seq=S)
    sb_qkv = _proj(hn, w_sb, [sb_gain], functools.partial(_epi_sb_qkv, dh=dh),
                   rows=rows, tm=tm_all, tn=sb_w, out_dtype=BF16, name="proj_sb_qkv")
    gates = _proj(hn, half_w_gate, [], _epi_sigmoid_of_double, rows=S, tm=tm_x, tn=1024, out_dtype=BF16,
                   name="proj_gates")

    o_dn = _gdn(qkv, gb, gb_t, dz, dn_out_norm_gain[0][None, :].astype(F32), seq=S, n_heads=H, dk=dk, dv=dv)
    o_sb = _stick_breaking(sb_qkv, seq=S, n_heads=sb_heads)

    h1 = _merge(x, o_dn, o_sb, gates, w_branch_dn[0].astype(BF16), w_branch_sb[0].astype(BF16),
                w_out[0].astype(BF16), tm=_row_tile(S, 512))
    return _ffn(h1, norm_ffn_gain[0][None, :].astype(F32), w_ffn_in[0].astype(BF16), w_ffn_out[0].astype(BF16),
                tm=_row_tile(S, 512))
```

```python
import functools

import jax
import jax.numpy as jnp
from jax import lax
from jax.experimental import pallas as pl
from jax.experimental.pallas import tpu as pltpu

F32 = jnp.float32
BF16 = jnp.bfloat16

N_META = 16
CHUNK = 64
BLK = 128
GDN_STEP = 256
META_ROWS = GDN_STEP
N_PAD = META_ROWS - N_META
RMS_EPS = 1e-6
L2_EPS = 1e-6
NEG_BIG = -1e30
FF_CHUNK = 256
SB_GROUP = 16
SB_EXIT_SUM = 105.0
SB_HEAD = 32
VMEM_LIMIT = 56 * 1024 * 1024

_NT = (((1,), (1,)), ((), ()))
_TN = (((0,), (0,)), ((), ()))


def _params(*sem):
    return pltpu.CompilerParams(dimension_semantics=sem, vmem_limit_bytes=VMEM_LIMIT)


def _dot(a, b, dims=None):
    if dims is None:
        return jnp.dot(a, b, preferred_element_type=F32)
    return lax.dot_general(a, b, dims, preferred_element_type=F32)


def _bdot(a, b, dims=None):
    return _dot(a.astype(BF16), b.astype(BF16), dims)


def _split(x, n):
    parts = []
    for _ in range(n - 1):
        p = x.astype(BF16)
        parts.append(p)
        x = x - p.astype(F32)
    parts.append(x.astype(BF16))
    return parts


def _sigmoid(x):
    return 1.0 / (1.0 + jnp.exp(-x))


def _softplus(x):
    return jnp.maximum(x, 0.0) + jnp.log(1.0 + jnp.exp(-jnp.abs(x)))


def _row_tile(n, cap=1152):
    best = 16
    for t in range(16, min(n, cap) + 1, 16):
        if n % t == 0:
            best = t
    return best


def _rms(v, gain):
    ms = jnp.mean(v * v, axis=-1, keepdims=True)
    return v * lax.rsqrt(ms + RMS_EPS) * gain


def _prenorm_kernel(x_ref, gain_ref, o_ref):
    o_ref[0] = _rms(x_ref[0], gain_ref[...]).astype(o_ref.dtype)


def _prenorm_meta_kernel(hn_ref, meta_ref, gain_ref, o_ref):
    del hn_ref
    o_ref[0] = _rms(meta_ref[...], gain_ref[...]).astype(o_ref.dtype)


def _prenorm(x, meta_blk, gain):
    B, S, D = x.shape
    tm = _row_tile(S, 1024)
    hn = pl.pallas_call(
        _prenorm_kernel,
        grid=(B, S // tm),
        in_specs=[
            pl.BlockSpec((1, tm, D), lambda b, i: (b, i, 0)),
            pl.BlockSpec((1, D), lambda b, i: (0, 0)),
        ],
        out_specs=pl.BlockSpec((1, tm, D), lambda b, i: (b, i, 0)),
        out_shape=jax.ShapeDtypeStruct((B, S + META_ROWS, D), BF16),
        compiler_params=_params("parallel", "parallel"),
        name="prenorm",
    )(x, gain)
    return pl.pallas_call(
        _prenorm_meta_kernel,
        grid=(B,),
        in_specs=[
            pl.BlockSpec(memory_space=pl.ANY),
            pl.BlockSpec((META_ROWS, D), lambda b: (0, 0)),
            pl.BlockSpec((1, D), lambda b: (0, 0)),
        ],
        out_specs=pl.BlockSpec((1, META_ROWS, D), lambda b: (b, S // META_ROWS, 0)),
        out_shape=jax.ShapeDtypeStruct((B, S + META_ROWS, D), BF16),
        input_output_aliases={0: 0},
        compiler_params=_params("parallel"),
        name="prenorm_meta",
    )(hn, meta_blk, gain)


def _proj_kernel(h_ref, wt_ref, *rest, epilogue):
    *aux, o_ref = rest
    acc = _dot(h_ref[0], wt_ref[...], _NT)
    epilogue(acc, aux, o_ref)


def _proj(hn, wt, aux, epilogue, *, rows, tm, tn, out_dtype, name):
    B, _, D = hn.shape
    N = wt.shape[0]
    tn = min(tn, N)
    assert N % tn == 0 and rows % tm == 0
    assert all(a.shape == (1, N) for a in aux)
    aux_specs = [pl.BlockSpec((1, tn), lambda b, i, j: (0, j)) for a in aux]
    return pl.pallas_call(
        functools.partial(_proj_kernel, epilogue=epilogue),
        grid=(B, rows // tm, N // tn),
        in_specs=[
            pl.BlockSpec((1, tm, D), lambda b, i, j: (b, i, 0)),
            pl.BlockSpec((tn, D), lambda b, i, j: (j, 0)),
        ] + aux_specs,
        out_specs=pl.BlockSpec((1, tm, tn), lambda b, i, j: (b, i, j)),
        out_shape=jax.ShapeDtypeStruct((B, rows, N), out_dtype),
        compiler_params=_params("parallel", "parallel", "arbitrary"),
        name=name,
    )(hn, wt, *aux)


def _epi_copy(acc, aux, o_ref):
    o_ref[0] = acc.astype(o_ref.dtype)


def _epi_silu_of_double(h, aux, o_ref):
    o_ref[0] = (h + h * jnp.tanh(h)).astype(o_ref.dtype)


def _epi_sigmoid_of_double(h, aux, o_ref):
    o_ref[0] = (0.5 * jnp.tanh(h) + 0.5).astype(o_ref.dtype)


def _decay_beta_kernel(h_ref, wt_ref, alog_ref, dtb_ref, o_ref, ot_ref, *, n_heads, seq, tm):
    i = pl.program_id(1)
    acc = _dot(h_ref[0], wt_ref[...], _NT)
    col = lax.broadcasted_iota(jnp.int32, acc.shape, 1)
    row = i * tm + lax.broadcasted_iota(jnp.int32, acc.shape, 0)
    g = -jnp.exp(alog_ref[...]) * _softplus(acc + dtb_ref[...])
    out = jnp.where(col < n_heads, g, _sigmoid(acc))
    is_pad = (row >= seq) & (row < seq + N_PAD)
    out = jnp.where(is_pad, 0.0, out)
    o_ref[0] = out
    ot_ref[0] = out.T


def _decay_beta(hn, wt, a_log, dt_bias, *, n_heads, seq):
    B, rows, D = hn.shape
    tm = rows
    vec = pl.BlockSpec((1, 128), lambda b, i: (0, 0))
    return pl.pallas_call(
        functools.partial(_decay_beta_kernel, n_heads=n_heads, seq=seq, tm=tm),
        grid=(B, rows // tm),
        in_specs=[pl.BlockSpec((1, tm, D), lambda b, i: (b, i, 0)), pl.BlockSpec((128, D), lambda b, i: (0, 0)), vec, vec],
        out_specs=[pl.BlockSpec((1, tm, 128), lambda b, i: (b, i, 0)), pl.BlockSpec((1, 128, tm), lambda b, i: (b, 0, i))],
        out_shape=[jax.ShapeDtypeStruct((B, rows, 128), F32), jax.ShapeDtypeStruct((B, 128, rows), F32)],
        compiler_params=_params("parallel", "parallel"),
        name="proj_decay_beta",
    )(hn, wt, a_log, dt_bias)


def _epi_sb_qkv(acc, aux, o_ref, *, dh):
    (gain_ref,) = aux
    j = pl.program_id(2)
    mul = jnp.where(j < 2, 1.0, 0.0)
    add = jnp.where(j < 2, 0.0, 1.0)
    tm = acc.shape[0]
    rc = _row_tile(tm, 96)
    for c in range(0, acc.shape[1], dh):
        for r0 in range(0, tm, rc):
            y = acc[r0:r0 + rc, c:c + dh]
            inv = lax.rsqrt(jnp.mean(y * y, axis=-1, keepdims=True) + RMS_EPS)
            o_ref[0, r0:r0 + rc, c:c + dh] = (y * (inv * mul + add) * gain_ref[:, c:c + dh]).astype(o_ref.dtype)


def _gdn_qkv_kernel(h_ref, wt_ref, cw_ref, mt_ref, o_ref, tail_ref, *, tm, rc, dk, qk_tiles):
    i = pl.program_id(1)
    j = pl.program_id(2)

    @pl.when((i == 0) & (j == 0))
    def _():
        tail_ref[...] = jnp.zeros(tail_ref.shape, F32)

    acc = _dot(h_ref[0], wt_ref[...], _NT)
    halo = jnp.where(i == 0, mt_ref[...], tail_ref[j])
    tail_ref[j] = acc[tm - 8:tm, :]

    mul = jnp.where(j < qk_tiles, dk ** -0.5, jnp.where(j < 2 * qk_tiles, 1.0, 0.0))
    add = jnp.where(j < 2 * qk_tiles, 0.0, 1.0)
    for c in range(0, acc.shape[1], dk):
        for r0 in range(0, tm, rc):
            if r0 == 0:
                xe = jnp.concatenate([halo[:, c:c + dk], acc[:rc, c:c + dk]], axis=0)
            else:
                xe = acc[r0 - 8:r0 + rc, c:c + dk]
            h = cw_ref[3:4, c:c + dk] * xe[8:]
            for tap in range(3):
                h = h + cw_ref[tap:tap + 1, c:c + dk] * pltpu.roll(xe, 3 - tap, axis=0)[8:]
            a = h + h * jnp.tanh(h)
            inv = lax.rsqrt(jnp.sum(a * a, axis=-1, keepdims=True) + L2_EPS)
            o_ref[0, r0:r0 + rc, c:c + dk] = (a * (inv * mul + add)).astype(o_ref.dtype)


def _gdn_qkv(hn, wt, conv_w, meta_tail, *, qk_w, dk, tm):
    B, rows, D = hn.shape
    N = wt.shape[0]
    tn = min(1024, qk_w)
    assert qk_w % tn == 0 and N % tn == 0 and tn % dk == 0 and rows % tm == 0
    rc = _row_tile(tm, 64)
    return pl.pallas_call(
        functools.partial(_gdn_qkv_kernel, tm=tm, rc=rc, dk=dk, qk_tiles=qk_w // tn),
        grid=(B, rows // tm, N // tn),
        in_specs=[
            pl.BlockSpec((1, tm, D), lambda b, i, j: (b, i, 0)),
            pl.BlockSpec((tn, D), lambda b, i, j: (j, 0)),
            pl.BlockSpec((4, tn), lambda b, i, j: (0, j)),
            pl.BlockSpec((8, tn), lambda b, i, j: (0, j)),
        ],
        out_specs=pl.BlockSpec((1, tm, tn), lambda b, i, j: (b, i, j)),
        out_shape=jax.ShapeDtypeStruct((B, rows, N), BF16),
        scratch_shapes=[pltpu.VMEM((N // tn, 8, tn), F32)],
        compiler_params=_params("parallel", "arbitrary", "arbitrary"),
        name="proj_gdn_qkv",
    )(hn, wt, conv_w, meta_tail)


def _inv_unit_lower(Ls, ixj):
    n = Ls[0].shape[0]
    eye = (ixj == 0).astype(F32)
    Xs = [eye - jnp.where(ixj == 1, L, 0.0) for L in Ls]
    h = 2
    while h < n:
        join = (ixj >= h) & (ixj < 2 * h)
        Xb = [X.astype(BF16) for X in Xs]
        XC = [_dot(xb, jnp.where(join, L, 0.0).astype(BF16)) for xb, L in zip(Xb, Ls)]
        Xs = [X - _dot(xc.astype(BF16), xb) for X, xb, xc in zip(Xs, Xb, XC)]
        h *= 2
    return Xs


def _gdn_kernel(x_ref, gb_ref, grow_ref, dz_ref, gain_ref, o_ref, s_ref, *, n_heads, dk, dv, cps):
    C = CHUNK
    qk_w = n_heads * dk
    heads = range(n_heads)
    subs = range(cps)
    items = [(s, h) for s in subs for h in heads]
    rows = lambda s: slice(s * C, (s + 1) * C)

    @pl.when(pl.program_id(1) == 0)
    def _():
        s_ref[...] = jnp.zeros(s_ref.shape, F32)

    ri = lax.broadcasted_iota(jnp.int32, (C, C), 0)
    ci = lax.broadcasted_iota(jnp.int32, (C, C), 1)
    ixj = ri ^ ci
    incl = ri >= ci
    strict = ri > ci

    tri = incl.astype(BF16)
    tri_t = (ri <= ci).astype(BF16)
    gb = [gb_ref[0, rows(s), :] for s in subs]
    gcum_c = [sum(_dot(tri, p) for p in _split(gb[s], 3)) for s in subs]
    gcum_r = [sum(_dot(p, tri_t) for p in _split(grow_ref[0, :, rows(s)], 3)) for s in subs]

    qb = {(s, h): x_ref[0, rows(s), h * dk:(h + 1) * dk] for s, h in items}
    kb16 = {(s, h): x_ref[0, rows(s), qk_w + h * dk:qk_w + (h + 1) * dk] for s, h in items}
    q = {i: qb[i].astype(F32) for i in items}
    k = {i: kb16[i].astype(F32) for i in items}
    v = {(s, h): x_ref[0, rows(s), 2 * qk_w + h * dv:2 * qk_w + (h + 1) * dv].astype(F32) for s, h in items}
    beta = {(s, h): gb[s][:, n_heads + h:n_heads + h + 1] for s, h in items}
    gc = {(s, h): gcum_c[s][:, h:h + 1] for s, h in items}
    g_last = {(s, h): gcum_c[s][C - 1:C, h:h + 1] for s, h in items}
    decay = {(s, h): jnp.exp(jnp.where(incl, gc[s, h] - gcum_r[s][h:h + 1, :], NEG_BIG))
             for s, h in items}
    kb = {i: k[i] * beta[i] for i in items}
    kq = {i: _dot(jnp.concatenate([kb[i].astype(BF16), qb[i]], axis=0), kb16[i], _NT) for i in items}
    lower = [jnp.where(strict, kq[i][:C] * decay[i], 0.0) for i in items]
    attn = {i: (kq[i][C:] * decay[i]).astype(BF16) for i in items}

    X = dict(zip(items, _inv_unit_lower(lower, ixj)))
    e_g = {i: jnp.exp(gc[i]) for i in items}
    sol = {i: _bdot(X[i], jnp.concatenate([v[i] * beta[i], kb[i] * e_g[i]], axis=1)) for i in items}
    wq = {i: jnp.concatenate([sol[i][:, dv:], q[i] * e_g[i]], axis=0).astype(BF16) for i in items}
    k_dec = {i: (k[i] * jnp.exp(g_last[i] - gc[i])).astype(BF16) for i in items}
    e_last = {i: jnp.exp(g_last[i]) for i in items}

    S = [s_ref[h] for h in heads]
    o = {}
    for s in subs:
        ws_qs = [_dot(wq[s, h], S[h].astype(BF16)) for h in heads]
        v_new = [(sol[s, h][:, :dv] - ws_qs[h][:C]).astype(BF16) for h in heads]
        for h in heads:
            o[s, h] = ws_qs[h][C:] + _dot(attn[s, h], v_new[h])
        S = [S[h] * e_last[s, h] + _dot(k_dec[s, h], v_new[h], _TN) for h in heads]
    for h in heads:
        s_ref[h] = S[h]

    for s, h in items:
        ms = jnp.mean(o[s, h] * o[s, h], axis=-1, keepdims=True)
        on = o[s, h] * lax.rsqrt(ms + RMS_EPS) * gain_ref[...]
        dz = dz_ref[0, rows(s), h * dv:(h + 1) * dv].astype(F32)
        o_ref[0, rows(s), h * dv:(h + 1) * dv] = (on * dz).astype(o_ref.dtype)


def _gdn(qkv, gb, grow, dz, gain, *, seq, n_heads, dk, dv):
    B = qkv.shape[0]
    W = qkv.shape[2]
    cps = GDN_STEP // CHUNK
    meta_blk = seq // GDN_STEP

    def tl(b, t):
        return (b, jnp.where(t == 0, meta_blk, t - 1), 0)

    def xr(b, t):
        return (b, jnp.maximum(t - 1, 0), 0)

    return pl.pallas_call(
        functools.partial(_gdn_kernel, n_heads=n_heads, dk=dk, dv=dv, cps=cps),
        grid=(B, seq // GDN_STEP + 1),
        in_specs=[
            pl.BlockSpec((1, GDN_STEP, W), tl),
            pl.BlockSpec((1, GDN_STEP, gb.shape[2]), tl),
            pl.BlockSpec((1, -(-n_heads // 8) * 8, GDN_STEP), lambda b, t: (b, 0, jnp.where(t == 0, meta_blk, t - 1))),
            pl.BlockSpec((1, GDN_STEP, n_heads * dv), xr),
            pl.BlockSpec((1, dv), lambda b, t: (0, 0)),
        ],
        out_specs=pl.BlockSpec((1, GDN_STEP, n_heads * dv), xr),
        out_shape=jax.ShapeDtypeStruct((B, seq, n_heads * dv), BF16),
        scratch_shapes=[pltpu.VMEM((n_heads, dk, dv), F32)],
        compiler_params=_params("parallel", "arbitrary"),
        name="gdn",
    )(qkv, gb, grow, dz, gain)


def _sb_kernel(q_ref, k_ref, v_ref, o_ref, acc_ref, later_ref, *, meta_start, group):
    i0 = pl.program_id(2) * group
    chains = range(group)
    ri = lax.broadcasted_iota(jnp.int32, (BLK, BLK), 0)
    ci = lax.broadcasted_iota(jnp.int32, (BLK, BLK), 1)
    diag_bias = jnp.where(ci < ri, 0.0, NEG_BIG)
    lane = lax.broadcasted_iota(jnp.int32, (1, BLK), 1)
    meta_bias = jnp.where(lane >= BLK - N_META, 0.0, NEG_BIG)
    r2 = lax.broadcasted_iota(jnp.int32, (2 * BLK, 2 * BLK), 0) & (BLK - 1)
    c2 = lax.broadcasted_iota(jnp.int32, (2 * BLK, 2 * BLK), 1)
    tail2 = ((r2 > c2) | (c2 >= BLK)).astype(BF16)

    def trip(jbs, biases, first, n_rows):
        starts = [pl.multiple_of(jnp.where(jb <= 0, meta_start, (jb - 1) * BLK), BLK) for jb in jbs]
        z = [_dot(q_ref[0, c * BLK:c * BLK + n_rows, :], k_ref[0, pl.ds(starts[c], BLK), :], _NT) + biases[c]
             for c in chains]
        sp = [_softplus(z[c]) for c in chains]
        sums = [_dot(jnp.concatenate(_split(sp[c], 2), axis=1), tail2) for c in chains]
        laters = []
        for c in chains:
            if first:
                later = sums[c][:, BLK:]
                acc_ref[c, :n_rows, :] = _dot(jnp.exp(z[c] - sp[c] - sums[c][:, :BLK]).astype(BF16),
                                              v_ref[0, pl.ds(starts[c], BLK), :])
            else:
                prev = later_ref[c, :n_rows, :]
                later = prev + sums[c][:, BLK:]
                acc_ref[c, :n_rows, :] += _dot(jnp.exp(z[c] - sp[c] - (prev + sums[c][:, :BLK])).astype(BF16),
                                               v_ref[0, pl.ds(starts[c], BLK), :])
            later_ref[c, :n_rows, :] = later
            laters.append(later)
        return laters

    def pending(laters, t_next):
        m = jnp.full((8, BLK), -NEG_BIG, F32)
        for c, later in enumerate(laters):
            col = jnp.min(later.reshape(later.shape[0] // 8, 8, BLK), axis=0)
            m = jnp.minimum(m, jnp.where(i0 + c + 1 - t_next >= 0, col, -NEG_BIG))
        return jnp.min(m)

    def walk(t0, low0, n_rows, watch):
        def body(carry):
            t, _ = carry
            jbs = [i0 + c + 1 - t for c in chains]
            biases = [jnp.where(jb >= 1, 0.0, jnp.where(jb == 0, meta_bias, NEG_BIG)) for jb in jbs]
            laters = trip(jbs, biases, False, n_rows)
            return t + 1, pending([later[watch] for later in laters], t + 1)

        return lax.while_loop(lambda carry: carry[1] <= SB_EXIT_SUM, body, (t0, low0))[0]

    head, rest = slice(0, SB_HEAD), slice(SB_HEAD, BLK)
    laters = trip([i0 + c + 1 for c in chains], [diag_bias] * group, True, BLK)
    t = walk(jnp.int32(1), pending([later[rest] for later in laters], 1), BLK, rest)
    walk(t, pending([later_ref[c, head, :] for c in chains], t), SB_HEAD, head)
    for c in range(group):
        o_ref[0, c * BLK:(c + 1) * BLK, :] = acc_ref[c].astype(o_ref.dtype)


def _stick_breaking(qkv, *, seq, n_heads):
    B, rows, _ = qkv.shape
    n_xblk = seq // BLK
    group = max(g for g in range(1, SB_GROUP + 1) if n_xblk % g == 0)
    return pl.pallas_call(
        functools.partial(_sb_kernel, meta_start=rows - BLK, group=group),
        grid=(B, n_heads, n_xblk // group),
        in_specs=[
            pl.BlockSpec((1, group * BLK, BLK), lambda b, h, g: (b, g, h)),
            pl.BlockSpec((1, rows, BLK), lambda b, h, g: (b, 0, n_heads + h)),
            pl.BlockSpec((1, rows, BLK), lambda b, h, g: (b, 0, 2 * n_heads + h)),
        ],
        out_specs=pl.BlockSpec((1, group * BLK, BLK), lambda b, h, g: (b, g, h)),
        out_shape=jax.ShapeDtypeStruct((B, seq, n_heads * BLK), BF16),
        scratch_shapes=[
            pltpu.VMEM((group, BLK, BLK), F32),
            pltpu.VMEM((group, BLK, BLK), F32),
        ],
        compiler_params=_params("parallel", "parallel", "arbitrary"),
        name="stick_breaking",
    )(qkv, qkv, qkv)


def _merge_kernel(x_ref, odn_ref, osb_ref, gate_ref, wdn_ref, wsb_ref, wo_ref, o_ref):
    D = x_ref.shape[2]
    a = _dot(odn_ref[0], wdn_ref[...])
    b = _dot(osb_ref[0], wsb_ref[...])
    merged = gate_ref[0, :, :D].astype(F32) * a + gate_ref[0, :, D:].astype(F32) * b
    o_ref[0] = x_ref[0] + _dot(merged.astype(BF16), wo_ref[...])


def _merge(x, o_dn, o_sb, gates, w_dn, w_sb, w_o, *, tm):
    B, S, D = x.shape
    full = lambda a: pl.BlockSpec(a.shape, lambda b, i: (0,) * a.ndim)
    rows = lambda a: pl.BlockSpec((1, tm, a.shape[2]), lambda b, i: (b, i, 0))
    return pl.pallas_call(
        _merge_kernel,
        grid=(B, S // tm),
        in_specs=[rows(x), rows(o_dn), rows(o_sb), rows(gates), full(w_dn), full(w_sb), full(w_o)],
        out_specs=rows(x),
        out_shape=jax.ShapeDtypeStruct((B, S, D), F32),
        compiler_params=_params("parallel", "parallel"),
        name="merge",
    )(x, o_dn, o_sb, gates, w_dn, w_sb, w_o)


def _ffn_kernel(h_ref, gain_ref, wi_ref, wo_ref, o_ref):
    d_ff = wo_ref.shape[0]
    h = h_ref[0]
    hn = _rms(h, gain_ref[...]).astype(BF16)
    o_ref[0] = h
    for c in range(0, d_ff, FF_CHUNK):
        g = _dot(hn, wi_ref[:, c:c + FF_CHUNK])
        u = _dot(hn, wi_ref[:, d_ff + c:d_ff + c + FF_CHUNK])
        act = (g * _sigmoid(g) * u).astype(BF16)
        o_ref[0] += _dot(act, wo_ref[c:c + FF_CHUNK, :])


def _ffn(h, gain, w_in, w_out, *, tm):
    B, S, D = h.shape
    full = lambda a: pl.BlockSpec(a.shape, lambda b, i: (0,) * a.ndim)
    rows = pl.BlockSpec((1, tm, D), lambda b, i: (b, i, 0))
    return pl.pallas_call(
        _ffn_kernel,
        grid=(B, S // tm),
        in_specs=[rows, full(gain), full(w_in), full(w_out)],
        out_specs=rows,
        out_shape=jax.ShapeDtypeStruct((B, S, D), F32),
        compiler_params=_params("parallel", "parallel"),
        name="ffn",
    )(h, gain, w_in, w_out)


def kernel(x, meta_tokens, norm_mix_gain, w_in, conv_q, conv_k, conv_v, dn_a_log, dn_dt_bias,
           dn_out_norm_gain, sb_q_norm_gain, sb_k_norm_gain, w_branch_dn, w_branch_sb, w_out,
           norm_ffn_gain, w_ffn_in, w_ffn_out):
    B, S, D = x.shape
    assert norm_mix_gain.shape[0] == 1, "one layer: the meta rows of the stream are not carried to a next layer"
    assert meta_tokens.shape[0] == N_META and S % GDN_STEP == 0
    H = dn_a_log.shape[-1]
    qk_w, v_w = conv_q.shape[-1], conv_v.shape[-1]
    dk, dv = qk_w // H, v_w // H
    dh = sb_q_norm_gain.shape[-1]
    sb_w = w_branch_sb.shape[-2]
    sb_heads = sb_w // dh
    d_ff = w_ffn_out.shape[-2]
    assert dk % 128 == 0 and dv % 128 == 0 and dh == BLK and 2 * H <= 128 and d_ff % FF_CHUNK == 0
    rows = S + META_ROWS

    wt = jnp.swapaxes(w_in[0], 0, 1)
    o_z = 2 * qk_w + v_w
    o_a = o_z + v_w
    o_s = o_a + 2 * H
    o_g = o_s + 3 * sb_w
    w_dqkv = wt[:o_z].astype(BF16)
    half_w_dz = (0.5 * wt[o_z:o_a]).astype(BF16)
    w_ab = jnp.pad(wt[o_a:o_s], ((0, 128 - 2 * H), (0, 0))).astype(BF16)
    w_sb = wt[o_s:o_g].astype(BF16)
    half_w_gate = (0.5 * wt[o_g:]).astype(BF16)
    lane_pad = lambda a: jnp.pad(a.astype(F32), (0, 128 - a.shape[0]))[None, :]
    a_log = lane_pad(dn_a_log[0])
    dt_bias = lane_pad(dn_dt_bias[0])
    sb_gain = jnp.concatenate([jnp.tile(sb_q_norm_gain[0] * (dh ** -0.5), sb_heads),
                               jnp.tile(sb_k_norm_gain[0], sb_heads),
                               jnp.ones((sb_w,), F32)])[None, :].astype(F32)
    half_conv_w = 0.5 * jnp.concatenate([conv_q[0], conv_k[0], conv_v[0]], axis=1).astype(F32)
    meta_blk = jnp.pad(meta_tokens.astype(F32), ((N_PAD, 0), (0, 0)))

    hn = _prenorm(x, meta_blk, norm_mix_gain[0][None, :].astype(F32))

    tm_all = _row_tile(rows)
    tm_x = _row_tile(S, 1024)
    meta_proj = _proj(hn[:1, S + N_PAD:], w_dqkv, [], _epi_copy, rows=N_META, tm=N_META, tn=1024,
                      out_dtype=F32, name="proj_meta_rows")
    qkv = _gdn_qkv(hn, w_dqkv, half_conv_w, meta_proj[0, N_META - 8:], qk_w=qk_w, dk=dk, tm=tm_all)
    dz = _proj(hn, half_w_dz, [], _epi_silu_of_double, rows=S, tm=tm_x, tn=1024, out_dtype=BF16, name="proj_dz")
    gb, gb_t = _decay_beta(hn, w_ab, a_log, dt_bias, n_heads=H, seq=S)
    sb_qkv = _proj(hn, w_sb, [sb_gain], functools.partial(_epi_sb_qkv, dh=dh),
                   rows=rows, tm=tm_all, tn=sb_w, out_dtype=BF16, name="proj_sb_qkv")
    gates = _proj(hn, half_w_gate, [], _epi_sigmoid_of_double, rows=S, tm=tm_x, tn=1024, out_dtype=BF16,
                   name="proj_gates")

    o_dn = _gdn(qkv, gb, gb_t, dz, dn_out_norm_gain[0][None, :].astype(F32), seq=S, n_heads=H, dk=dk, dv=dv)
    o_sb = _stick_breaking(sb_qkv, seq=S, n_heads=sb_heads)

    h1 = _merge(x, o_dn, o_sb, gates, w_branch_dn[0].astype(BF16), w_branch_sb[0].astype(BF16),
                w_out[0].astype(BF16), tm=_row_tile(S, 512))
    return _ffn(h1, norm_ffn_gain[0][None, :].astype(F32), w_ffn_in[0].astype(BF16), w_ffn_out[0].astype(BF16),
                tm=_row_tile(S, 512))
```

```python
import functools

import jax
import jax.numpy as jnp
from jax import lax
from jax.experimental import pallas as pl
from jax.experimental.pallas import tpu as pltpu

F32 = jnp.float32
BF16 = jnp.bfloat16

N_META = 16
CHUNK = 64
BLK = 128
GDN_STEP = 256
META_ROWS = GDN_STEP
N_PAD = META_ROWS - N_META
RMS_EPS = 1e-6
L2_EPS = 1e-6
NEG_BIG = -1e30
FF_CHUNK = 256
SB_GROUP = 32
SB_EXIT_SUM = 105.0
SB_HEAD = 32
VMEM_LIMIT = 56 * 1024 * 1024

_NT = (((1,), (1,)), ((), ()))
_TN = (((0,), (0,)), ((), ()))


def _params(*sem):
    return pltpu.CompilerParams(dimension_semantics=sem, vmem_limit_bytes=VMEM_LIMIT)


def _dot(a, b, dims=None):
    if dims is None:
        return jnp.dot(a, b, preferred_element_type=F32)
    return lax.dot_general(a, b, dims, preferred_element_type=F32)


def _bdot(a, b, dims=None):
    return _dot(a.astype(BF16), b.astype(BF16), dims)


def _split(x, n):
    parts = []
    for _ in range(n - 1):
        p = x.astype(BF16)
        parts.append(p)
        x = x - p.astype(F32)
    parts.append(x.astype(BF16))
    return parts


def _sigmoid(x):
    return 1.0 / (1.0 + jnp.exp(-x))


def _softplus(x):
    return jnp.maximum(x, 0.0) + jnp.log(1.0 + jnp.exp(-jnp.abs(x)))


def _row_tile(n, cap=1152):
    best = 16
    for t in range(16, min(n, cap) + 1, 16):
        if n % t == 0:
            best = t
    return best


def _rms(v, gain):
    ms = jnp.mean(v * v, axis=-1, keepdims=True)
    return v * lax.rsqrt(ms + RMS_EPS) * gain


def _prenorm_kernel(x_ref, gain_ref, o_ref):
    o_ref[0] = _rms(x_ref[0], gain_ref[...]).astype(o_ref.dtype)


def _prenorm_meta_kernel(hn_ref, meta_ref, gain_ref, o_ref):
    del hn_ref
    o_ref[0] = _rms(meta_ref[...], gain_ref[...]).astype(o_ref.dtype)


def _prenorm(x, meta_blk, gain):
    B, S, D = x.shape
    tm = _row_tile(S, 1024)
    hn = pl.pallas_call(
        _prenorm_kernel,
        grid=(B, S // tm),
        in_specs=[
            pl.BlockSpec((1, tm, D), lambda b, i: (b, i, 0)),
            pl.BlockSpec((1, D), lambda b, i: (0, 0)),
        ],
        out_specs=pl.BlockSpec((1, tm, D), lambda b, i: (b, i, 0)),
        out_shape=jax.ShapeDtypeStruct((B, S + META_ROWS, D), BF16),
        compiler_params=_params("parallel", "parallel"),
        name="prenorm",
    )(x, gain)
    return pl.pallas_call(
        _prenorm_meta_kernel,
        grid=(B,),
        in_specs=[
            pl.BlockSpec(memory_space=pl.ANY),
            pl.BlockSpec((META_ROWS, D), lambda b: (0, 0)),
            pl.BlockSpec((1, D), lambda b: (0, 0)),
        ],
        out_specs=pl.BlockSpec((1, META_ROWS, D), lambda b: (b, S // META_ROWS, 0)),
        out_shape=jax.ShapeDtypeStruct((B, S + META_ROWS, D), BF16),
        input_output_aliases={0: 0},
        compiler_params=_params("parallel"),
        name="prenorm_meta",
    )(hn, meta_blk, gain)


def _proj_kernel(h_ref, wt_ref, *rest, epilogue):
    *aux, o_ref = rest
    acc = _dot(h_ref[0], wt_ref[...], _NT)
    epilogue(acc, aux, o_ref)


def _proj(hn, wt, aux, epilogue, *, rows, tm, tn, out_dtype, name):
    B, _, D = hn.shape
    N = wt.shape[0]
    tn = min(tn, N)
    assert N % tn == 0 and rows % tm == 0
    assert all(a.shape == (1, N) for a in aux)
    aux_specs = [pl.BlockSpec((1, tn), lambda b, i, j: (0, j)) for a in aux]
    return pl.pallas_call(
        functools.partial(_proj_kernel, epilogue=epilogue),
        grid=(B, rows // tm, N // tn),
        in_specs=[
            pl.BlockSpec((1, tm, D), lambda b, i, j: (b, i, 0)),
            pl.BlockSpec((tn, D), lambda b, i, j: (j, 0)),
        ] + aux_specs,
        out_specs=pl.BlockSpec((1, tm, tn), lambda b, i, j: (b, i, j)),
        out_shape=jax.ShapeDtypeStruct((B, rows, N), out_dtype),
        compiler_params=_params("parallel", "parallel", "arbitrary"),
        name=name,
    )(hn, wt, *aux)


def _epi_copy(acc, aux, o_ref):
    o_ref[0] = acc.astype(o_ref.dtype)


def _epi_silu_of_double(h, aux, o_ref):
    o_ref[0] = (h + h * jnp.tanh(h)).astype(o_ref.dtype)


def _epi_sigmoid_of_double(h, aux, o_ref):
    o_ref[0] = (0.5 * jnp.tanh(h) + 0.5).astype(o_ref.dtype)


def _decay_beta_kernel(h_ref, wt_ref, alog_ref, dtb_ref, o_ref, ot_ref, *, n_heads, seq, tm):
    i = pl.program_id(1)
    acc = _dot(h_ref[0], wt_ref[...], _NT)
    col = lax.broadcasted_iota(jnp.int32, acc.shape, 1)
    row = i * tm + lax.broadcasted_iota(jnp.int32, acc.shape, 0)
    g = -jnp.exp(alog_ref[...]) * _softplus(acc + dtb_ref[...])
    out = jnp.where(col < n_heads, g, _sigmoid(acc))
    is_pad = (row >= seq) & (row < seq + N_PAD)
    out = jnp.where(is_pad, 0.0, out)
    o_ref[0] = out
    ot_ref[0] = out.T


def _decay_beta(hn, wt, a_log, dt_bias, *, n_heads, seq):
    B, rows, D = hn.shape
    tm = rows
    vec = pl.BlockSpec((1, 128), lambda b, i: (0, 0))
    return pl.pallas_call(
        functools.partial(_decay_beta_kernel, n_heads=n_heads, seq=seq, tm=tm),
        grid=(B, rows // tm),
        in_specs=[pl.BlockSpec((1, tm, D), lambda b, i: (b, i, 0)), pl.BlockSpec((128, D), lambda b, i: (0, 0)), vec, vec],
        out_specs=[pl.BlockSpec((1, tm, 128), lambda b, i: (b, i, 0)), pl.BlockSpec((1, 128, tm), lambda b, i: (b, 0, i))],
        out_shape=[jax.ShapeDtypeStruct((B, rows, 128), F32), jax.ShapeDtypeStruct((B, 128, rows), F32)],
        compiler_params=_params("parallel", "parallel"),
        name="proj_decay_beta",
    )(hn, wt, a_log, dt_bias)


def _epi_sb_qkv(acc, aux, o_ref, *, dh):
    (gain_ref,) = aux
    j = pl.program_id(2)
    mul = jnp.where(j < 2, 1.0, 0.0)
    add = jnp.where(j < 2, 0.0, 1.0)
    tm = acc.shape[0]
    rc = _row_tile(tm, 96)
    for c in range(0, acc.shape[1], dh):
        for r0 in range(0, tm, rc):
            y = acc[r0:r0 + rc, c:c + dh]
            inv = lax.rsqrt(jnp.mean(y * y, axis=-1, keepdims=True) + RMS_EPS)
            o_ref[0, r0:r0 + rc, c:c + dh] = (y * (inv * mul + add) * gain_ref[:, c:c + dh]).astype(o_ref.dtype)


def _gdn_qkv_kernel(h_ref, wt_ref, cw_ref, mt_ref, o_ref, tail_ref, *, tm, rc, dk, qk_tiles):
    i = pl.program_id(1)
    j = pl.program_id(2)

    @pl.when((i == 0) & (j == 0))
    def _():
        tail_ref[...] = jnp.zeros(tail_ref.shape, F32)

    acc = _dot(h_ref[0], wt_ref[...], _NT)
    halo = jnp.where(i == 0, mt_ref[...], tail_ref[j])
    tail_ref[j] = acc[tm - 8:tm, :]

    mul = jnp.where(j < qk_tiles, dk ** -0.5, jnp.where(j < 2 * qk_tiles, 1.0, 0.0))
    add = jnp.where(j < 2 * qk_tiles, 0.0, 1.0)
    for c in range(0, acc.shape[1], dk):
        for r0 in range(0, tm, rc):
            if r0 == 0:
                xe = jnp.concatenate([halo[:, c:c + dk], acc[:rc, c:c + dk]], axis=0)
            else:
                xe = acc[r0 - 8:r0 + rc, c:c + dk]
            h = cw_ref[3:4, c:c + dk] * xe[8:]
            for tap in range(3):
                h = h + cw_ref[tap:tap + 1, c:c + dk] * pltpu.roll(xe, 3 - tap, axis=0)[8:]
            a = h + h * jnp.tanh(h)
            inv = lax.rsqrt(jnp.sum(a * a, axis=-1, keepdims=True) + L2_EPS)
            o_ref[0, r0:r0 + rc, c:c + dk] = (a * (inv * mul + add)).astype(o_ref.dtype)


def _gdn_qkv(hn, wt, conv_w, meta_tail, *, qk_w, dk, tm):
    B, rows, D = hn.shape
    N = wt.shape[0]
    tn = min(1024, qk_w)
    assert qk_w % tn == 0 and N % tn == 0 and tn % dk == 0 and rows % tm == 0
    rc = _row_tile(tm, 64)
    return pl.pallas_call(
        functools.partial(_gdn_qkv_kernel, tm=tm, rc=rc, dk=dk, qk_tiles=qk_w // tn),
        grid=(B, rows // tm, N // tn),
        in_specs=[
            pl.BlockSpec((1, tm, D), lambda b, i, j: (b, i, 0)),
            pl.BlockSpec((tn, D), lambda b, i, j: (j, 0)),
            pl.BlockSpec((4, tn), lambda b, i, j: (0, j)),
            pl.BlockSpec((8, tn), lambda b, i, j: (0, j)),
        ],
        out_specs=pl.BlockSpec((1, tm, tn), lambda b, i, j: (b, i, j)),
        out_shape=jax.ShapeDtypeStruct((B, rows, N), BF16),
        scratch_shapes=[pltpu.VMEM((N // tn, 8, tn), F32)],
        compiler_params=_params("parallel", "arbitrary", "arbitrary"),
        name="proj_gdn_qkv",
    )(hn, wt, conv_w, meta_tail)


def _inv_unit_lower(Ls, ixj):
    n = Ls[0].shape[0]
    eye = (ixj == 0).astype(F32)
    Xs = [eye - jnp.where(ixj == 1, L, 0.0) for L in Ls]
    h = 2
    while h < n:
        join = (ixj >= h) & (ixj < 2 * h)
        Xb = [X.astype(BF16) for X in Xs]
        XC = [_dot(xb, jnp.where(join, L, 0.0).astype(BF16)) for xb, L in zip(Xb, Ls)]
        Xs = [X - _dot(xc.astype(BF16), xb) for X, xb, xc in zip(Xs, Xb, XC)]
        h *= 2
    return Xs


def _gdn_kernel(x_ref, gb_ref, grow_ref, dz_ref, gain_ref, o_ref, s_ref, *, n_heads, dk, dv, cps):
    C = CHUNK
    qk_w = n_heads * dk
    heads = range(n_heads)
    subs = range(cps)
    items = [(s, h) for s in subs for h in heads]
    rows = lambda s: slice(s * C, (s + 1) * C)

    @pl.when(pl.program_id(1) == 0)
    def _():
        s_ref[...] = jnp.zeros(s_ref.shape, F32)

    ri = lax.broadcasted_iota(jnp.int32, (C, C), 0)
    ci = lax.broadcasted_iota(jnp.int32, (C, C), 1)
    ixj = ri ^ ci
    incl = ri >= ci
    strict = ri > ci

    tri = incl.astype(BF16)
    tri_t = (ri <= ci).astype(BF16)
    gb = [gb_ref[0, rows(s), :] for s in subs]
    gcum_c = [sum(_dot(tri, p) for p in _split(gb[s], 3)) for s in subs]
    gcum_r = [sum(_dot(p, tri_t) for p in _split(grow_ref[0, :, rows(s)], 3)) for s in subs]

    qb = {(s, h): x_ref[0, rows(s), h * dk:(h + 1) * dk] for s, h in items}
    kb16 = {(s, h): x_ref[0, rows(s), qk_w + h * dk:qk_w + (h + 1) * dk] for s, h in items}
    q = {i: qb[i].astype(F32) for i in items}
    k = {i: kb16[i].astype(F32) for i in items}
    v = {(s, h): x_ref[0, rows(s), 2 * qk_w + h * dv:2 * qk_w + (h + 1) * dv].astype(F32) for s, h in items}
    beta = {(s, h): gb[s][:, n_heads + h:n_heads + h + 1] for s, h in items}
    gc = {(s, h): gcum_c[s][:, h:h + 1] for s, h in items}
    g_last = {(s, h): gcum_c[s][C - 1:C, h:h + 1] for s, h in items}
    decay = {(s, h): jnp.exp(jnp.where(incl, gc[s, h] - gcum_r[s][h:h + 1, :], NEG_BIG))
             for s, h in items}
    kb = {i: k[i] * beta[i] for i in items}
    kq = {i: _dot(jnp.concatenate([kb[i].astype(BF16), qb[i]], axis=0), kb16[i], _NT) for i in items}
    lower = [jnp.where(strict, kq[i][:C] * decay[i], 0.0) for i in items]
    attn = {i: (kq[i][C:] * decay[i]).astype(BF16) for i in items}

    X = dict(zip(items, _inv_unit_lower(lower, ixj)))
    e_g = {i: jnp.exp(gc[i]) for i in items}
    sol = {i: _bdot(X[i], jnp.concatenate([v[i] * beta[i], kb[i] * e_g[i]], axis=1)) for i in items}
    wq = {i: jnp.concatenate([sol[i][:, dv:], q[i] * e_g[i]], axis=0).astype(BF16) for i in items}
    k_dec = {i: (k[i] * jnp.exp(g_last[i] - gc[i])).astype(BF16) for i in items}
    e_last = {i: jnp.exp(g_last[i]) for i in items}

    S = [s_ref[h] for h in heads]
    o = {}
    for s in subs:
        ws_qs = [_dot(wq[s, h], S[h].astype(BF16)) for h in heads]
        v_new = [(sol[s, h][:, :dv] - ws_qs[h][:C]).astype(BF16) for h in heads]
        for h in heads:
            o[s, h] = ws_qs[h][C:] + _dot(attn[s, h], v_new[h])
        S = [S[h] * e_last[s, h] + _dot(k_dec[s, h], v_new[h], _TN) for h in heads]
    for h in heads:
        s_ref[h] = S[h]

    for s, h in items:
        ms = jnp.mean(o[s, h] * o[s, h], axis=-1, keepdims=True)
        on = o[s, h] * lax.rsqrt(ms + RMS_EPS) * gain_ref[...]
        dz = dz_ref[0, rows(s), h * dv:(h + 1) * dv].astype(F32)
        o_ref[0, rows(s), h * dv:(h + 1) * dv] = (on * dz).astype(o_ref.dtype)


def _gdn(qkv, gb, grow, dz, gain, *, seq, n_heads, dk, dv):
    B = qkv.shape[0]
    W = qkv.shape[2]
    cps = GDN_STEP // CHUNK
    meta_blk = seq // GDN_STEP

    def tl(b, t):
        return (b, jnp.where(t == 0, meta_blk, t - 1), 0)

    def xr(b, t):
        return (b, jnp.maximum(t - 1, 0), 0)

    return pl.pallas_call(
        functools.partial(_gdn_kernel, n_heads=n_heads, dk=dk, dv=dv, cps=cps),
        grid=(B, seq // GDN_STEP + 1),
        in_specs=[
            pl.BlockSpec((1, GDN_STEP, W), tl),
            pl.BlockSpec((1, GDN_STEP, gb.shape[2]), tl),
            pl.BlockSpec((1, -(-n_heads // 8) * 8, GDN_STEP), lambda b, t: (b, 0, jnp.where(t == 0, meta_blk, t - 1))),
            pl.BlockSpec((1, GDN_STEP, n_heads * dv), xr),
            pl.BlockSpec((1, dv), lambda b, t: (0, 0)),
        ],
        out_specs=pl.BlockSpec((1, GDN_STEP, n_heads * dv), xr),
        out_shape=jax.ShapeDtypeStruct((B, seq, n_heads * dv), BF16),
        scratch_shapes=[pltpu.VMEM((n_heads, dk, dv), F32)],
        compiler_params=_params("parallel", "arbitrary"),
        name="gdn",
    )(qkv, gb, grow, dz, gain)


def _sb_kernel(q_ref, k_ref, v_ref, o_ref, acc_ref, later_ref, *, meta_start, group):
    i0 = pl.program_id(2) * group
    chains = range(group)
    ri = lax.broadcasted_iota(jnp.int32, (BLK, BLK), 0)
    ci = lax.broadcasted_iota(jnp.int32, (BLK, BLK), 1)
    diag_bias = jnp.where(ci < ri, 0.0, NEG_BIG)
    lane = lax.broadcasted_iota(jnp.int32, (1, BLK), 1)
    meta_bias = jnp.where(lane >= BLK - N_META, 0.0, NEG_BIG)
    r2 = lax.broadcasted_iota(jnp.int32, (2 * BLK, 2 * BLK), 0) & (BLK - 1)
    c2 = lax.broadcasted_iota(jnp.int32, (2 * BLK, 2 * BLK), 1)
    tail2 = ((r2 > c2) | (c2 >= BLK)).astype(BF16)

    def trip(jbs, biases, first, n_rows):
        starts = [pl.multiple_of(jnp.where(jb <= 0, meta_start, (jb - 1) * BLK), BLK) for jb in jbs]
        z = [_dot(q_ref[0, c * BLK:c * BLK + n_rows, :], k_ref[0, pl.ds(starts[c], BLK), :], _NT) + biases[c]
             for c in chains]
        sp = [_softplus(z[c]) for c in chains]
        sums = [_dot(jnp.concatenate(_split(sp[c], 2), axis=1), tail2) for c in chains]
        laters = []
        for c in chains:
            if first:
                later = sums[c][:, BLK:]
                acc_ref[c, :n_rows, :] = _dot(jnp.exp(z[c] - sp[c] - sums[c][:, :BLK]).astype(BF16),
                                              v_ref[0, pl.ds(starts[c], BLK), :])
            else:
                prev = later_ref[c, :n_rows, :]
                later = prev + sums[c][:, BLK:]
                acc_ref[c, :n_rows, :] += _dot(jnp.exp(z[c] - sp[c] - (prev + sums[c][:, :BLK])).astype(BF16),
                                               v_ref[0, pl.ds(starts[c], BLK), :])
            later_ref[c, :n_rows, :] = later
            laters.append(later)
        return laters

    def pending(laters, t_next):
        m = jnp.full((8, BLK), -NEG_BIG, F32)
        for c, later in enumerate(laters):
            col = jnp.min(later.reshape(later.shape[0] // 8, 8, BLK), axis=0)
            m = jnp.minimum(m, jnp.where(i0 + c + 1 - t_next >= 0, col, -NEG_BIG))
        return jnp.min(m)

    def walk(t0, low0, n_rows, watch):
        def body(carry):
            t, _ = carry
            jbs = [i0 + c + 1 - t for c in chains]
            biases = [jnp.where(jb >= 1, 0.0, jnp.where(jb == 0, meta_bias, NEG_BIG)) for jb in jbs]
            laters = trip(jbs, biases, False, n_rows)
            return t + 1, pending([later[watch] for later in laters], t + 1)

        return lax.while_loop(lambda carry: carry[1] <= SB_EXIT_SUM, body, (t0, low0))[0]

    head, rest = slice(0, SB_HEAD), slice(SB_HEAD, BLK)
    laters = trip([i0 + c + 1 for c in chains], [diag_bias] * group, True, BLK)
    t = walk(jnp.int32(1), pending([later[rest] for later in laters], 1), BLK, rest)
    walk(t, pending([later_ref[c, head, :] for c in chains], t), SB_HEAD, head)
    for c in range(group):
        o_ref[0, c * BLK:(c + 1) * BLK, :] = acc_ref[c].astype(o_ref.dtype)


def _stick_breaking(qkv, *, seq, n_heads):
    B, rows, _ = qkv.shape
    n_xblk = seq // BLK
    group = max(g for g in range(1, SB_GROUP + 1) if n_xblk % g == 0)
    return pl.pallas_call(
        functools.partial(_sb_kernel, meta_start=rows - BLK, group=group),
        grid=(B, n_heads, n_xblk // group),
        in_specs=[
            pl.BlockSpec((1, group * BLK, BLK), lambda b, h, g: (b, g, h)),
            pl.BlockSpec((1, rows, BLK), lambda b, h, g: (b, 0, n_heads + h)),
            pl.BlockSpec((1, rows, BLK), lambda b, h, g: (b, 0, 2 * n_heads + h)),
        ],
        out_specs=pl.BlockSpec((1, group * BLK, BLK), lambda b, h, g: (b, g, h)),
        out_shape=jax.ShapeDtypeStruct((B, seq, n_heads * BLK), BF16),
        scratch_shapes=[
            pltpu.VMEM((group, BLK, BLK), F32),
            pltpu.VMEM((group, BLK, BLK), F32),
        ],
        compiler_params=_params("parallel", "parallel", "arbitrary"),
        name="stick_breaking",
    )(qkv, qkv, qkv)


def _merge_kernel(x_ref, odn_ref, osb_ref, gate_ref, wdn_ref, wsb_ref, wo_ref, o_ref):
    D = x_ref.shape[2]
    a = _dot(odn_ref[0], wdn_ref[...])
    b = _dot(osb_ref[0], wsb_ref[...])
    merged = gate_ref[0, :, :D].astype(F32) * a + gate_ref[0, :, D:].astype(F32) * b
    o_ref[0] = x_ref[0] + _dot(merged.astype(BF16), wo_ref[...])


def _merge(x, o_dn, o_sb, gates, w_dn, w_sb, w_o, *, tm):
    B, S, D = x.shape
    full = lambda a: pl.BlockSpec(a.shape, lambda b, i: (0,) * a.ndim)
    rows = lambda a: pl.BlockSpec((1, tm, a.shape[2]), lambda b, i: (b, i, 0))
    return pl.pallas_call(
        _merge_kernel,
        grid=(B, S // tm),
        in_specs=[rows(x), rows(o_dn), rows(o_sb), rows(gates), full(w_dn), full(w_sb), full(w_o)],
        out_specs=rows(x),
        out_shape=jax.ShapeDtypeStruct((B, S, D), F32),
        compiler_params=_params("parallel", "parallel"),
        name="merge",
    )(x, o_dn, o_sb, gates, w_dn, w_sb, w_o)


def _ffn_kernel(h_ref, gain_ref, wi_ref, wo_ref, o_ref):
    d_ff = wo_ref.shape[0]
    h = h_ref[0]
    hn = _rms(h, gain_ref[...]).astype(BF16)
    o_ref[0] = h
    for c in range(0, d_ff, FF_CHUNK):
        g = _dot(hn, wi_ref[:, c:c + FF_CHUNK])
        u = _dot(hn, wi_ref[:, d_ff + c:d_ff + c + FF_CHUNK])
        act = (g * _sigmoid(g) * u).astype(BF16)
        o_ref[0] += _dot(act, wo_ref[c:c + FF_CHUNK, :])


def _ffn(h, gain, w_in, w_out, *, tm):
    B, S, D = h.shape
    full = lambda a: pl.BlockSpec(a.shape, lambda b, i: (0,) * a.ndim)
    rows = pl.BlockSpec((1, tm, D), lambda b, i: (b, i, 0))
    return pl.pallas_call(
        _ffn_kernel,
        grid=(B, S // tm),
        in_specs=[rows, full(gain), full(w_in), full(w_out)],
        out_specs=rows,
        out_shape=jax.ShapeDtypeStruct((B, S, D), F32),
        compiler_params=_params("parallel", "parallel"),
        name="ffn",
    )(h, gain, w_in, w_out)


def kernel(x, meta_tokens, norm_mix_gain, w_in, conv_q, conv_k, conv_v, dn_a_log, dn_dt_bias,
           dn_out_norm_gain, sb_q_norm_gain, sb_k_norm_gain, w_branch_dn, w_branch_sb, w_out,
           norm_ffn_gain, w_ffn_in, w_ffn_out):
    B, S, D = x.shape
    assert norm_mix_gain.shape[0] == 1, "one layer: the meta rows of the stream are not carried to a next layer"
    assert meta_tokens.shape[0] == N_META and S % GDN_STEP == 0
    H = dn_a_log.shape[-1]
    qk_w, v_w = conv_q.shape[-1], conv_v.shape[-1]
    dk, dv = qk_w // H, v_w // H
    dh = sb_q_norm_gain.shape[-1]
    sb_w = w_branch_sb.shape[-2]
    sb_heads = sb_w // dh
    d_ff = w_ffn_out.shape[-2]
    assert dk % 128 == 0 and dv % 128 == 0 and dh == BLK and 2 * H <= 128 and d_ff % FF_CHUNK == 0
    rows = S + META_ROWS

    wt = jnp.swapaxes(w_in[0], 0, 1)
    o_z = 2 * qk_w + v_w
    o_a = o_z + v_w
    o_s = o_a + 2 * H
    o_g = o_s + 3 * sb_w
    w_dqkv = wt[:o_z].astype(BF16)
    half_w_dz = (0.5 * wt[o_z:o_a]).astype(BF16)
    w_ab = jnp.pad(wt[o_a:o_s], ((0, 128 - 2 * H), (0, 0))).astype(BF16)
    w_sb = wt[o_s:o_g].astype(BF16)
    half_w_gate = (0.5 * wt[o_g:]).astype(BF16)
    lane_pad = lambda a: jnp.pad(a.astype(F32), (0, 128 - a.shape[0]))[None, :]
    a_log = lane_pad(dn_a_log[0])
    dt_bias = lane_pad(dn_dt_bias[0])
    sb_gain = jnp.concatenate([jnp.tile(sb_q_norm_gain[0] * (dh ** -0.5), sb_heads),
                               jnp.tile(sb_k_norm_gain[0], sb_heads),
                               jnp.ones((sb_w,), F32)])[None, :].astype(F32)
    half_conv_w = 0.5 * jnp.concatenate([conv_q[0], conv_k[0], conv_v[0]], axis=1).astype(F32)
    meta_blk = jnp.pad(meta_tokens.astype(F32), ((N_PAD, 0), (0, 0)))

    hn = _prenorm(x, meta_blk, norm_mix_gain[0][None, :].astype(F32))

    tm_all = _row_tile(rows)
    tm_x = _row_tile(S, 1024)
    meta_proj = _proj(hn[:1, S + N_PAD:], w_dqkv, [], _epi_copy, rows=N_META, tm=N_META, tn=1024,
                      out_dtype=F32, name="proj_meta_rows")
    qkv = _gdn_qkv(hn, w_dqkv, half_conv_w, meta_proj[0, N_META - 8:], qk_w=qk_w, dk=dk, tm=tm_all)
    dz = _proj(hn, half_w_dz, [], _epi_silu_of_double, rows=S, tm=tm_x, tn=2048, out_dtype=BF16, name="proj_dz")
    gb, gb_t = _decay_beta(hn, w_ab, a_log, dt_bias, n_heads=H, seq=S)
    sb_qkv = _proj(hn, w_sb, [sb_gain], functools.partial(_epi_sb_qkv, dh=dh),
                   rows=rows, tm=tm_all, tn=sb_w, out_dtype=BF16, name="proj_sb_qkv")
    gates = _proj(hn, half_w_gate, [], _epi_sigmoid_of_double, rows=S, tm=tm_x, tn=2048, out_dtype=BF16,
                   name="proj_gates")

    o_dn = _gdn(qkv, gb, gb_t, dz, dn_out_norm_gain[0][None, :].astype(F32), seq=S, n_heads=H, dk=dk, dv=dv)
    o_sb = _stick_breaking(sb_qkv, seq=S, n_heads=sb_heads)

    h1 = _merge(x, o_dn, o_sb, gates, w_branch_dn[0].astype(BF16), w_branch_sb[0].astype(BF16),
                w_out[0].astype(BF16), tm=_row_tile(S, 512))
    return _ffn(h1, norm_ffn_gain[0][None, :].astype(F32), w_ffn_in[0].astype(BF16), w_ffn_out[0].astype(BF16),
                tm=_row_tile(S, 512))
```

```python
import functools

import jax
import jax.numpy as jnp
from jax import lax
from jax.experimental import pallas as pl
from jax.experimental.pallas import tpu as pltpu

F32 = jnp.float32
BF16 = jnp.bfloat16

N_META = 16
CHUNK = 64
BLK = 128
GDN_STEP = 256
META_ROWS = GDN_STEP
N_PAD = META_ROWS - N_META
RMS_EPS = 1e-6
L2_EPS = 1e-6
NEG_BIG = -1e30
FF_CHUNK = 256
SB_GROUP = 32
SB_EXIT_SUM = 105.0
SB_HEAD = 32
VMEM_LIMIT = 56 * 1024 * 1024

_NT = (((1,), (1,)), ((), ()))
_TN = (((0,), (0,)), ((), ()))


def _params(*sem):
    return pltpu.CompilerParams(dimension_semantics=sem, vmem_limit_bytes=VMEM_LIMIT)


def _dot(a, b, dims=None):
    if dims is None:
        return jnp.dot(a, b, preferred_element_type=F32)
    return lax.dot_general(a, b, dims, preferred_element_type=F32)


def _bdot(a, b, dims=None):
    return _dot(a.astype(BF16), b.astype(BF16), dims)


def _split(x, n):
    parts = []
    for _ in range(n - 1):
        p = x.astype(BF16)
        parts.append(p)
        x = x - p.astype(F32)
    parts.append(x.astype(BF16))
    return parts


def _sigmoid(x):
    return 1.0 / (1.0 + jnp.exp(-x))


def _softplus(x):
    return jnp.maximum(x, 0.0) + jnp.log(1.0 + jnp.exp(-jnp.abs(x)))


def _row_tile(n, cap=1152):
    best = 16
    for t in range(16, min(n, cap) + 1, 16):
        if n % t == 0:
            best = t
    return best


def _rms(v, gain):
    ms = jnp.mean(v * v, axis=-1, keepdims=True)
    return v * lax.rsqrt(ms + RMS_EPS) * gain


def _prenorm_kernel(x_ref, gain_ref, o_ref):
    o_ref[0] = _rms(x_ref[0], gain_ref[...]).astype(o_ref.dtype)


def _prenorm_meta_kernel(hn_ref, meta_ref, gain_ref, o_ref):
    del hn_ref
    o_ref[0] = _rms(meta_ref[...], gain_ref[...]).astype(o_ref.dtype)


def _prenorm(x, meta_blk, gain):
    B, S, D = x.shape
    tm = _row_tile(S, 1024)
    hn = pl.pallas_call(
        _prenorm_kernel,
        grid=(B, S // tm),
        in_specs=[
            pl.BlockSpec((1, tm, D), lambda b, i: (b, i, 0)),
            pl.BlockSpec((1, D), lambda b, i: (0, 0)),
        ],
        out_specs=pl.BlockSpec((1, tm, D), lambda b, i: (b, i, 0)),
        out_shape=jax.ShapeDtypeStruct((B, S + META_ROWS, D), BF16),
        compiler_params=_params("parallel", "parallel"),
        name="prenorm",
    )(x, gain)
    return pl.pallas_call(
        _prenorm_meta_kernel,
        grid=(B,),
        in_specs=[
            pl.BlockSpec(memory_space=pl.ANY),
            pl.BlockSpec((META_ROWS, D), lambda b: (0, 0)),
            pl.BlockSpec((1, D), lambda b: (0, 0)),
        ],
        out_specs=pl.BlockSpec((1, META_ROWS, D), lambda b: (b, S // META_ROWS, 0)),
        out_shape=jax.ShapeDtypeStruct((B, S + META_ROWS, D), BF16),
        input_output_aliases={0: 0},
        compiler_params=_params("parallel"),
        name="prenorm_meta",
    )(hn, meta_blk, gain)


def _proj_kernel(h_ref, wt_ref, *rest, epilogue):
    *aux, o_ref = rest
    acc = _dot(h_ref[0], wt_ref[...], _NT)
    epilogue(acc, aux, o_ref)


def _proj(hn, wt, aux, epilogue, *, rows, tm, tn, out_dtype, name):
    B, _, D = hn.shape
    N = wt.shape[0]
    tn = min(tn, N)
    assert N % tn == 0 and rows % tm == 0
    assert all(a.shape == (1, N) for a in aux)
    aux_specs = [pl.BlockSpec((1, tn), lambda b, i, j: (0, j)) for a in aux]
    return pl.pallas_call(
        functools.partial(_proj_kernel, epilogue=epilogue),
        grid=(B, rows // tm, N // tn),
        in_specs=[
            pl.BlockSpec((1, tm, D), lambda b, i, j: (b, i, 0)),
            pl.BlockSpec((tn, D), lambda b, i, j: (j, 0)),
        ] + aux_specs,
        out_specs=pl.BlockSpec((1, tm, tn), lambda b, i, j: (b, i, j)),
        out_shape=jax.ShapeDtypeStruct((B, rows, N), out_dtype),
        compiler_params=_params("parallel", "parallel", "arbitrary"),
        name=name,
    )(hn, wt, *aux)


def _epi_copy(acc, aux, o_ref):
    o_ref[0] = acc.astype(o_ref.dtype)


def _epi_silu_of_double(h, aux, o_ref):
    o_ref[0] = (h + h * jnp.tanh(h)).astype(o_ref.dtype)


def _epi_sigmoid_of_double(h, aux, o_ref):
    o_ref[0] = (0.5 * jnp.tanh(h) + 0.5).astype(o_ref.dtype)


def _decay_beta_kernel(h_ref, wt_ref, alog_ref, dtb_ref, o_ref, ot_ref, *, n_heads, seq, tm):
    i = pl.program_id(1)
    acc = _dot(h_ref[0], wt_ref[...], _NT)
    col = lax.broadcasted_iota(jnp.int32, acc.shape, 1)
    row = i * tm + lax.broadcasted_iota(jnp.int32, acc.shape, 0)
    g = -jnp.exp(alog_ref[...]) * _softplus(acc + dtb_ref[...])
    out = jnp.where(col < n_heads, g, _sigmoid(acc))
    is_pad = (row >= seq) & (row < seq + N_PAD)
    out = jnp.where(is_pad, 0.0, out)
    o_ref[0] = out
    ot_ref[0] = out.T


def _decay_beta(hn, wt, a_log, dt_bias, *, n_heads, seq):
    B, rows, D = hn.shape
    tm = rows
    vec = pl.BlockSpec((1, 128), lambda b, i: (0, 0))
    return pl.pallas_call(
        functools.partial(_decay_beta_kernel, n_heads=n_heads, seq=seq, tm=tm),
        grid=(B, rows // tm),
        in_specs=[pl.BlockSpec((1, tm, D), lambda b, i: (b, i, 0)), pl.BlockSpec((128, D), lambda b, i: (0, 0)), vec, vec],
        out_specs=[pl.BlockSpec((1, tm, 128), lambda b, i: (b, i, 0)), pl.BlockSpec((1, 128, tm), lambda b, i: (b, 0, i))],
        out_shape=[jax.ShapeDtypeStruct((B, rows, 128), F32), jax.ShapeDtypeStruct((B, 128, rows), F32)],
        compiler_params=_params("parallel", "parallel"),
        name="proj_decay_beta",
    )(hn, wt, a_log, dt_bias)


def _epi_head_rms(acc, aux, o_ref, *, dh):
    (gain_ref,) = aux
    tm = acc.shape[0]
    rc = _row_tile(tm, 96)
    for c in range(0, acc.shape[1], dh):
        for r0 in range(0, tm, rc):
            y = acc[r0:r0 + rc, c:c + dh]
            inv = lax.rsqrt(jnp.mean(y * y, axis=-1, keepdims=True) + RMS_EPS)
            o_ref[0, r0:r0 + rc, c:c + dh] = (y * inv * gain_ref[:, c:c + dh]).astype(o_ref.dtype)


def _gdn_qkv_kernel(h_ref, wt_ref, cw_ref, mt_ref, o_ref, tail_ref, *, tm, rc, dk, qk_tiles):
    i = pl.program_id(1)
    j = pl.program_id(2)

    @pl.when((i == 0) & (j == 0))
    def _():
        tail_ref[...] = jnp.zeros(tail_ref.shape, F32)

    acc = _dot(h_ref[0], wt_ref[...], _NT)
    halo = jnp.where(i == 0, mt_ref[...], tail_ref[j])
    tail_ref[j] = acc[tm - 8:tm, :]

    mul = jnp.where(j < qk_tiles, dk ** -0.5, jnp.where(j < 2 * qk_tiles, 1.0, 0.0))
    add = jnp.where(j < 2 * qk_tiles, 0.0, 1.0)
    for c in range(0, acc.shape[1], dk):
        for r0 in range(0, tm, rc):
            if r0 == 0:
                xe = jnp.concatenate([halo[:, c:c + dk], acc[:rc, c:c + dk]], axis=0)
            else:
                xe = acc[r0 - 8:r0 + rc, c:c + dk]
            h = cw_ref[3:4, c:c + dk] * xe[8:]
            for tap in range(3):
                h = h + cw_ref[tap:tap + 1, c:c + dk] * pltpu.roll(xe, 3 - tap, axis=0)[8:]
            a = h + h * jnp.tanh(h)
            inv = lax.rsqrt(jnp.sum(a * a, axis=-1, keepdims=True) + L2_EPS)
            o_ref[0, r0:r0 + rc, c:c + dk] = (a * (inv * mul + add)).astype(o_ref.dtype)


def _gdn_qkv(hn, wt, conv_w, meta_tail, *, qk_w, dk, tm):
    B, rows, D = hn.shape
    N = wt.shape[0]
    tn = min(1024, qk_w)
    assert qk_w % tn == 0 and N % tn == 0 and tn % dk == 0 and rows % tm == 0
    rc = _row_tile(tm, 64)
    return pl.pallas_call(
        functools.partial(_gdn_qkv_kernel, tm=tm, rc=rc, dk=dk, qk_tiles=qk_w // tn),
        grid=(B, rows // tm, N // tn),
        in_specs=[
            pl.BlockSpec((1, tm, D), lambda b, i, j: (b, i, 0)),
            pl.BlockSpec((tn, D), lambda b, i, j: (j, 0)),
            pl.BlockSpec((4, tn), lambda b, i, j: (0, j)),
            pl.BlockSpec((8, tn), lambda b, i, j: (0, j)),
        ],
        out_specs=pl.BlockSpec((1, tm, tn), lambda b, i, j: (b, i, j)),
        out_shape=jax.ShapeDtypeStruct((B, rows, N), BF16),
        scratch_shapes=[pltpu.VMEM((N // tn, 8, tn), F32)],
        compiler_params=_params("parallel", "arbitrary", "arbitrary"),
        name="proj_gdn_qkv",
    )(hn, wt, conv_w, meta_tail)


def _inv_unit_lower(Ls, ixj):
    n = Ls[0].shape[0]
    eye = (ixj == 0).astype(F32)
    Xs = [eye - jnp.where(ixj == 1, L, 0.0) for L in Ls]
    h = 2
    while h < n:
        join = (ixj >= h) & (ixj < 2 * h)
        Xb = [X.astype(BF16) for X in Xs]
        XC = [_dot(xb, jnp.where(join, L, 0.0).astype(BF16)) for xb, L in zip(Xb, Ls)]
        Xs = [X - _dot(xc.astype(BF16), xb) for X, xb, xc in zip(Xs, Xb, XC)]
        h *= 2
    return Xs


def _gdn_kernel(x_ref, gb_ref, grow_ref, dz_ref, gain_ref, o_ref, s_ref, *, n_heads, dk, dv, cps):
    C = CHUNK
    qk_w = n_heads * dk
    heads = range(n_heads)
    subs = range(cps)
    items = [(s, h) for s in subs for h in heads]
    rows = lambda s: slice(s * C, (s + 1) * C)

    @pl.when(pl.program_id(1) == 0)
    def _():
        s_ref[...] = jnp.zeros(s_ref.shape, F32)

    ri = lax.broadcasted_iota(jnp.int32, (C, C), 0)
    ci = lax.broadcasted_iota(jnp.int32, (C, C), 1)
    ixj = ri ^ ci
    incl = ri >= ci
    strict = ri > ci

    tri = incl.astype(BF16)
    tri_t = (ri <= ci).astype(BF16)
    gb = [gb_ref[0, rows(s), :] for s in subs]
    gcum_c = [sum(_dot(tri, p) for p in _split(gb[s], 3)) for s in subs]
    gcum_r = [sum(_dot(p, tri_t) for p in _split(grow_ref[0, :, rows(s)], 3)) for s in subs]

    qb = {(s, h): x_ref[0, rows(s), h * dk:(h + 1) * dk] for s, h in items}
    kb16 = {(s, h): x_ref[0, rows(s), qk_w + h * dk:qk_w + (h + 1) * dk] for s, h in items}
    q = {i: qb[i].astype(F32) for i in items}
    k = {i: kb16[i].astype(F32) for i in items}
    v = {(s, h): x_ref[0, rows(s), 2 * qk_w + h * dv:2 * qk_w + (h + 1) * dv].astype(F32) for s, h in items}
    beta = {(s, h): gb[s][:, n_heads + h:n_heads + h + 1] for s, h in items}
    gc = {(s, h): gcum_c[s][:, h:h + 1] for s, h in items}
    g_last = {(s, h): gcum_c[s][C - 1:C, h:h + 1] for s, h in items}
    decay = {(s, h): jnp.exp(jnp.where(incl, gc[s, h] - gcum_r[s][h:h + 1, :], NEG_BIG))
             for s, h in items}
    kb = {i: k[i] * beta[i] for i in items}
    kq = {i: _dot(jnp.concatenate([kb[i].astype(BF16), qb[i]], axis=0), kb16[i], _NT) for i in items}
    lower = [jnp.where(strict, kq[i][:C] * decay[i], 0.0) for i in items]
    attn = {i: (kq[i][C:] * decay[i]).astype(BF16) for i in items}

    X = dict(zip(items, _inv_unit_lower(lower, ixj)))
    e_g = {i: jnp.exp(gc[i]) for i in items}
    sol = {i: _bdot(X[i], jnp.concatenate([v[i] * beta[i], kb[i] * e_g[i]], axis=1)) for i in items}
    wq = {i: jnp.concatenate([sol[i][:, dv:], q[i] * e_g[i]], axis=0).astype(BF16) for i in items}
    k_dec = {i: (k[i] * jnp.exp(g_last[i] - gc[i])).astype(BF16) for i in items}
    e_last = {i: jnp.exp(g_last[i]) for i in items}

    S = [s_ref[h] for h in heads]
    o = {}
    for s in subs:
        ws_qs = [_dot(wq[s, h], S[h].astype(BF16)) for h in heads]
        v_new = [(sol[s, h][:, :dv] - ws_qs[h][:C]).astype(BF16) for h in heads]
        for h in heads:
            o[s, h] = ws_qs[h][C:] + _dot(attn[s, h], v_new[h])
        S = [S[h] * e_last[s, h] + _dot(k_dec[s, h], v_new[h], _TN) for h in heads]
    for h in heads:
        s_ref[h] = S[h]

    for s, h in items:
        ms = jnp.mean(o[s, h] * o[s, h], axis=-1, keepdims=True)
        on = o[s, h] * lax.rsqrt(ms + RMS_EPS) * gain_ref[...]
        dz = dz_ref[0, rows(s), h * dv:(h + 1) * dv].astype(F32)
        o_ref[0, rows(s), h * dv:(h + 1) * dv] = (on * dz).astype(o_ref.dtype)


def _gdn(qkv, gb, grow, dz, gain, *, seq, n_heads, dk, dv):
    B = qkv.shape[0]
    W = qkv.shape[2]
    cps = GDN_STEP // CHUNK
    meta_blk = seq // GDN_STEP

    def tl(b, t):
        return (b, jnp.where(t == 0, meta_blk, t - 1), 0)

    def xr(b, t):
        return (b, jnp.maximum(t - 1, 0), 0)

    return pl.pallas_call(
        functools.partial(_gdn_kernel, n_heads=n_heads, dk=dk, dv=dv, cps=cps),
        grid=(B, seq // GDN_STEP + 1),
        in_specs=[
            pl.BlockSpec((1, GDN_STEP, W), tl),
            pl.BlockSpec((1, GDN_STEP, gb.shape[2]), tl),
            pl.BlockSpec((1, -(-n_heads // 8) * 8, GDN_STEP), lambda b, t: (b, 0, jnp.where(t == 0, meta_blk, t - 1))),
            pl.BlockSpec((1, GDN_STEP, n_heads * dv), xr),
            pl.BlockSpec((1, dv), lambda b, t: (0, 0)),
        ],
        out_specs=pl.BlockSpec((1, GDN_STEP, n_heads * dv), xr),
        out_shape=jax.ShapeDtypeStruct((B, seq, n_heads * dv), BF16),
        scratch_shapes=[pltpu.VMEM((n_heads, dk, dv), F32)],
        compiler_params=_params("parallel", "arbitrary"),
        name="gdn",
    )(qkv, gb, grow, dz, gain)


def _sb_kernel(q_ref, k_ref, v_ref, o_ref, acc_ref, later_ref, *, meta_start, group):
    i0 = pl.program_id(2) * group
    chains = range(group)
    ri = lax.broadcasted_iota(jnp.int32, (BLK, BLK), 0)
    ci = lax.broadcasted_iota(jnp.int32, (BLK, BLK), 1)
    diag_bias = jnp.where(ci < ri, 0.0, NEG_BIG)
    lane = lax.broadcasted_iota(jnp.int32, (1, BLK), 1)
    meta_bias = jnp.where(lane >= BLK - N_META, 0.0, NEG_BIG)
    r2 = lax.broadcasted_iota(jnp.int32, (2 * BLK, 2 * BLK), 0) & (BLK - 1)
    c2 = lax.broadcasted_iota(jnp.int32, (2 * BLK, 2 * BLK), 1)
    tail2 = ((r2 > c2) | (c2 >= BLK)).astype(BF16)

    def trip(jbs, biases, first, n_rows):
        starts = [pl.multiple_of(jnp.where(jb <= 0, meta_start, (jb - 1) * BLK), BLK) for jb in jbs]
        z = [_dot(q_ref[0, c * BLK:c * BLK + n_rows, :], k_ref[0, pl.ds(starts[c], BLK), :], _NT) + biases[c]
             for c in chains]
        sp = [_softplus(z[c]) for c in chains]
        sums = [_dot(jnp.concatenate(_split(sp[c], 2), axis=1), tail2) for c in chains]
        laters = []
        for c in chains:
            if first:
                later = sums[c][:, BLK:]
                acc_ref[c, :n_rows, :] = _dot(jnp.exp(z[c] - sp[c] - sums[c][:, :BLK]).astype(BF16),
                                              v_ref[0, pl.ds(starts[c], BLK), :])
            else:
                prev = later_ref[c, :n_rows, :]
                later = prev + sums[c][:, BLK:]
                acc_ref[c, :n_rows, :] += _dot(jnp.exp(z[c] - sp[c] - (prev + sums[c][:, :BLK])).astype(BF16),
                                               v_ref[0, pl.ds(starts[c], BLK), :])
            later_ref[c, :n_rows, :] = later
            laters.append(later)
        return laters

    def pending(laters, t_next):
        m = jnp.full((8, BLK), -NEG_BIG, F32)
        for c, later in enumerate(laters):
            col = jnp.min(later.reshape(later.shape[0] // 8, 8, BLK), axis=0)
            m = jnp.minimum(m, jnp.where(i0 + c + 1 - t_next >= 0, col, -NEG_BIG))
        return jnp.min(m)

    def walk(t0, low0, n_rows, watch):
        def body(carry):
            t, _ = carry
            jbs = [i0 + c + 1 - t for c in chains]
            biases = [jnp.where(jb >= 1, 0.0, jnp.where(jb == 0, meta_bias, NEG_BIG)) for jb in jbs]
            laters = trip(jbs, biases, False, n_rows)
            return t + 1, pending([later[watch] for later in laters], t + 1)

        return lax.while_loop(lambda carry: carry[1] <= SB_EXIT_SUM, body, (t0, low0))[0]

    head, rest = slice(0, SB_HEAD), slice(SB_HEAD, BLK)
    laters = trip([i0 + c + 1 for c in chains], [diag_bias] * group, True, BLK)
    t = walk(jnp.int32(1), pending([later[rest] for later in laters], 1), BLK, rest)
    walk(t, pending([later_ref[c, head, :] for c in chains], t), SB_HEAD, head)
    for c in range(group):
        o_ref[0, c * BLK:(c + 1) * BLK, :] = acc_ref[c].astype(o_ref.dtype)


def _stick_breaking(qk, v, *, seq, n_heads):
    B, rows, _ = qk.shape
    n_xblk = seq // BLK
    group = max(g for g in range(1, SB_GROUP + 1) if n_xblk % g == 0)
    return pl.pallas_call(
        functools.partial(_sb_kernel, meta_start=rows - BLK, group=group),
        grid=(B, n_heads, n_xblk // group),
        in_specs=[
            pl.BlockSpec((1, group * BLK, BLK), lambda b, h, g: (b, g, h)),
            pl.BlockSpec((1, rows, BLK), lambda b, h, g: (b, 0, n_heads + h)),
            pl.BlockSpec((1, rows, BLK), lambda b, h, g: (b, 0, h)),
        ],
        out_specs=pl.BlockSpec((1, group * BLK, BLK), lambda b, h, g: (b, g, h)),
        out_shape=jax.ShapeDtypeStruct((B, seq, n_heads * BLK), BF16),
        scratch_shapes=[
            pltpu.VMEM((group, BLK, BLK), F32),
            pltpu.VMEM((group, BLK, BLK), F32),
        ],
        compiler_params=_params("parallel", "parallel", "arbitrary"),
        name="stick_breaking",
    )(qk, qk, v)


def _merge_kernel(x_ref, odn_ref, osb_ref, gate_ref, wdn_ref, wsb_ref, wo_ref, o_ref):
    D = x_ref.shape[2]
    a = _dot(odn_ref[0], wdn_ref[...])
    b = _dot(osb_ref[0], wsb_ref[...])
    merged = gate_ref[0, :, :D].astype(F32) * a + gate_ref[0, :, D:].astype(F32) * b
    o_ref[0] = x_ref[0] + _dot(merged.astype(BF16), wo_ref[...])


def _merge(x, o_dn, o_sb, gates, w_dn, w_sb, w_o, *, tm):
    B, S, D = x.shape
    full = lambda a: pl.BlockSpec(a.shape, lambda b, i: (0,) * a.ndim)
    rows = lambda a: pl.BlockSpec((1, tm, a.shape[2]), lambda b, i: (b, i, 0))
    return pl.pallas_call(
        _merge_kernel,
        grid=(B, S // tm),
        in_specs=[rows(x), rows(o_dn), rows(o_sb), rows(gates), full(w_dn), full(w_sb), full(w_o)],
        out_specs=rows(x),
        out_shape=jax.ShapeDtypeStruct((B, S, D), F32),
        compiler_params=_params("parallel", "parallel"),
        name="merge",
    )(x, o_dn, o_sb, gates, w_dn, w_sb, w_o)


def _ffn_kernel(h_ref, gain_ref, wi_ref, wo_ref, o_ref):
    d_ff = wo_ref.shape[0]
    h = h_ref[0]
    hn = _rms(h, gain_ref[...]).astype(BF16)
    o_ref[0] = h
    for c in range(0, d_ff, FF_CHUNK):
        g = _dot(hn, wi_ref[:, c:c + FF_CHUNK])
        u = _dot(hn, wi_ref[:, d_ff + c:d_ff + c + FF_CHUNK])
        act = (g * _sigmoid(g) * u).astype(BF16)
        o_ref[0] += _dot(act, wo_ref[c:c + FF_CHUNK, :])


def _ffn(h, gain, w_in, w_out, *, tm):
    B, S, D = h.shape
    full = lambda a: pl.BlockSpec(a.shape, lambda b, i: (0,) * a.ndim)
    rows = pl.BlockSpec((1, tm, D), lambda b, i: (b, i, 0))
    return pl.pallas_call(
        _ffn_kernel,
        grid=(B, S // tm),
        in_specs=[rows, full(gain), full(w_in), full(w_out)],
        out_specs=rows,
        out_shape=jax.ShapeDtypeStruct((B, S, D), F32),
        compiler_params=_params("parallel", "parallel"),
        name="ffn",
    )(h, gain, w_in, w_out)


def kernel(x, meta_tokens, norm_mix_gain, w_in, conv_q, conv_k, conv_v, dn_a_log, dn_dt_bias,
           dn_out_norm_gain, sb_q_norm_gain, sb_k_norm_gain, w_branch_dn, w_branch_sb, w_out,
           norm_ffn_gain, w_ffn_in, w_ffn_out):
    B, S, D = x.shape
    assert norm_mix_gain.shape[0] == 1, "one layer: the meta rows of the stream are not carried to a next layer"
    assert meta_tokens.shape[0] == N_META and S % GDN_STEP == 0
    H = dn_a_log.shape[-1]
    qk_w, v_w = conv_q.shape[-1], conv_v.shape[-1]
    dk, dv = qk_w // H, v_w // H
    dh = sb_q_norm_gain.shape[-1]
    sb_w = w_branch_sb.shape[-2]
    sb_heads = sb_w // dh
    d_ff = w_ffn_out.shape[-2]
    assert dk % 128 == 0 and dv % 128 == 0 and dh == BLK and 2 * H <= 128 and d_ff % FF_CHUNK == 0
    rows = S + META_ROWS

    wt = jnp.swapaxes(w_in[0], 0, 1)
    o_z = 2 * qk_w + v_w
    o_a = o_z + v_w
    o_s = o_a + 2 * H
    o_g = o_s + 3 * sb_w
    w_dqkv = wt[:o_z].astype(BF16)
    half_w_dz = (0.5 * wt[o_z:o_a]).astype(BF16)
    w_ab = jnp.pad(wt[o_a:o_s], ((0, 128 - 2 * H), (0, 0))).astype(BF16)
    w_sb_qk = wt[o_s:o_s + 2 * sb_w].astype(BF16)
    w_sb_v = wt[o_s + 2 * sb_w:o_g].astype(BF16)
    half_w_gate = (0.5 * wt[o_g:]).astype(BF16)
    lane_pad = lambda a: jnp.pad(a.astype(F32), (0, 128 - a.shape[0]))[None, :]
    a_log = lane_pad(dn_a_log[0])
    dt_bias = lane_pad(dn_dt_bias[0])
    sb_gain = jnp.concatenate([jnp.tile(sb_q_norm_gain[0] * (dh ** -0.5), sb_heads),
                               jnp.tile(sb_k_norm_gain[0], sb_heads)])[None, :].astype(F32)
    half_conv_w = 0.5 * jnp.concatenate([conv_q[0], conv_k[0], conv_v[0]], axis=1).astype(F32)
    meta_blk = jnp.pad(meta_tokens.astype(F32), ((N_PAD, 0), (0, 0)))

    hn = _prenorm(x, meta_blk, norm_mix_gain[0][None, :].astype(F32))

    tm_all = _row_tile(rows)
    tm_x = _row_tile(S, 1024)
    meta_proj = _proj(hn[:1, S + N_PAD:], w_dqkv, [], _epi_copy, rows=N_META, tm=N_META, tn=1024,
                      out_dtype=F32, name="proj_meta_rows")
    qkv = _gdn_qkv(hn, w_dqkv, half_conv_w, meta_proj[0, N_META - 8:], qk_w=qk_w, dk=dk, tm=tm_all)
    dz = _proj(hn, half_w_dz, [], _epi_silu_of_double, rows=S, tm=tm_x, tn=2048, out_dtype=BF16, name="proj_dz")
    gb, gb_t = _decay_beta(hn, w_ab, a_log, dt_bias, n_heads=H, seq=S)
    sb_qk = _proj(hn, w_sb_qk, [sb_gain], functools.partial(_epi_head_rms, dh=dh),
                  rows=rows, tm=tm_all, tn=2 * sb_w, out_dtype=BF16, name="proj_sb_qk")
    sb_v = _proj(hn, w_sb_v, [], _epi_copy, rows=rows, tm=tm_all, tn=sb_w, out_dtype=BF16, name="proj_sb_v")
    gates = _proj(hn, half_w_gate, [], _epi_sigmoid_of_double, rows=S, tm=tm_x, tn=2048, out_dtype=BF16,
                   name="proj_gates")

    o_dn = _gdn(qkv, gb, gb_t, dz, dn_out_norm_gain[0][None, :].astype(F32), seq=S, n_heads=H, dk=dk, dv=dv)
    o_sb = _stick_breaking(sb_qk, sb_v, seq=S, n_heads=sb_heads)

    h1 = _merge(x, o_dn, o_sb, gates, w_branch_dn[0].astype(BF16), w_branch_sb[0].astype(BF16),
                w_out[0].astype(BF16), tm=_row_tile(S, 512))
    return _ffn(h1, norm_ffn_gain[0][None, :].astype(F32), w_ffn_in[0].astype(BF16), w_ffn_out[0].astype(BF16),
                tm=_row_tile(S, 512))
```
